```python
import jax, jax.numpy as jnp
from jax import lax
import numpy as np

D_MODEL = 1024
BATCH = 8
SEQ = 8192
DEPTH = 2

CHUNK = 64
SB_BLOCK = 128
EPS = 1e-6
D_FF = 2816
D_PLE = 256
SB_WIDTH = D_MODEL // 2
SB_HEAD_DIM = 64
SB_HEADS = SB_WIDTH // SB_HEAD_DIM
HG_WIDTH = D_MODEL // 4
HG_HEAD_DIM = 64
HG_HEADS = HG_WIDTH // HG_HEAD_DIM
GLA_WIDTH = D_MODEL // 4
GLA_HEADS = 4
GLA_VALUE_DIM = GLA_WIDTH // GLA_HEADS
GLA_KEY_DIM = GLA_VALUE_DIM // 2
GLA_GATE_RANK = 16
GLA_GATE_TAU = 16.0
D_MIX = SB_WIDTH + HG_WIDTH + GLA_WIDTH
IN_SPLITS = (SB_WIDTH, SB_WIDTH, SB_WIDTH,
             HG_WIDTH, HG_WIDTH, HG_WIDTH, HG_WIDTH,
             GLA_HEADS * GLA_KEY_DIM, GLA_HEADS * GLA_KEY_DIM,
             GLA_WIDTH, GLA_GATE_RANK, GLA_WIDTH)
D_IN = sum(IN_SPLITS)

kernel_name = "hybrid_sb_hgrn2_gla_macaron_ple"


def rms_norm(x, g):
    xf = x.astype(jnp.float32)
    y = xf * lax.rsqrt(jnp.mean(xf * xf, axis=-1, keepdims=True) + EPS)
    return (y * g.astype(jnp.float32)).astype(x.dtype)


def head_rms_norm(o, gain):
    o = o * lax.rsqrt(jnp.mean(o * o, axis=-1, keepdims=True) + EPS)
    return o.reshape(o.shape[0], o.shape[1], -1) * gain.astype(jnp.float32)


def swiglu(u, w_gate, w_up, w_down):
    return (jax.nn.silu(u @ w_gate) * (u @ w_up)) @ w_down


def stick_breaking_attention(q, k, v):
    T = q.shape[2]
    scale = q.shape[-1] ** -0.5
    outs = []
    for qb in range(T // SB_BLOCK):
        t0 = qb * SB_BLOCK
        t1 = t0 + SB_BLOCK
        ks = k[:, :, :t1]
        vs = v[:, :, :t1]
        z = jnp.einsum('bhqd,bhkd->bhqk', q[:, :, t0:t1], ks).astype(jnp.float32) * scale
        strict = jnp.arange(t1)[None, :] < jnp.arange(t0, t1)[:, None]
        log1m = jnp.where(strict, jax.nn.log_sigmoid(-z), 0.0)
        tail = lax.cumsum(log1m, axis=3, reverse=True) - log1m
        w = jnp.exp(jnp.where(strict, jax.nn.log_sigmoid(z) + tail, -jnp.inf))
        outs.append(jnp.einsum('bhqk,bhkd->bhqd', w.astype(v.dtype), vs))
    return jnp.concatenate(outs, axis=2)


def chunk_gated_linear_attention(q, k, v, log_a):
    B, T, H, dk = q.shape
    dv = v.shape[-1]
    n = T // CHUNK

    def to_chunks(z):
        return z.astype(jnp.float32).reshape(B, n, CHUNK, H, z.shape[-1]).transpose(1, 0, 3, 2, 4)

    qc, kc, vc, gc = to_chunks(q), to_chunks(k), to_chunks(v), to_chunks(log_a)
    causal = jnp.tril(jnp.ones((CHUNK, CHUNK), dtype=bool))

    def step(S, inp):
        qi, ki, vi, gi = inp
        b = jnp.cumsum(gi, axis=-2)
        o_inter = jnp.einsum('bhtd,bhde->bhte', qi * jnp.exp(b), S)
        diff = b[..., :, None, :] - b[..., None, :, :]
        decay = jnp.exp(jnp.where(causal[:, :, None], diff, -jnp.inf))
        att = jnp.einsum('bhtd,bhsd,bhtsd->bhts', qi, ki, decay)
        o_intra = jnp.einsum('bhts,bhse->bhte', att, vi)
        b_last = b[..., -1:, :]
        S_new = (jnp.exp(b_last[..., 0, :])[..., None] * S
                 + jnp.einsum('bhsd,bhse->bhde', ki * jnp.exp(b_last - b), vi))
        return S_new, o_inter + o_intra

    S0 = jnp.zeros((B, H, dk, dv), jnp.float32)
    _, o = lax.scan(step, S0, (qc, kc, vc, gc))
    return o.transpose(1, 0, 3, 2, 4).reshape(B, T, H, dv)


def token_mixing(u, w_in, w_out, lower_bound, hg_gain, gla_w_gate_up, gla_b_gate, gla_gain):
    B, T, _ = u.shape
    offs = [int(o) for o in np.cumsum(IN_SPLITS)[:-1]]
    (sq, sk, sv, hq, hf, hi, hg, gq, gk, gv, ga, gg) = jnp.split(u @ w_in, offs, axis=-1)

    def heads(z, h):
        return z.reshape(B, T, h, -1)

    to_bhtd = lambda z: heads(z, SB_HEADS).transpose(0, 2, 1, 3)
    sb = stick_breaking_attention(to_bhtd(sq), to_bhtd(sk), to_bhtd(sv))
    sb = sb.transpose(0, 2, 1, 3).reshape(B, T, SB_WIDTH)

    fl = hf.astype(jnp.float32)
    if lower_bound is None:
        log_f = jax.nn.log_sigmoid(fl)
    else:
        log_f = jnp.log(lower_bound + (1.0 - lower_bound) * jax.nn.sigmoid(fl))
    key_h = -jnp.expm1(log_f)
    o_h = chunk_gated_linear_attention(heads(jax.nn.silu(hq), HG_HEADS), heads(key_h, HG_HEADS),
                                       heads(hi, HG_HEADS), heads(log_f, HG_HEADS))
    hgrn = head_rms_norm(o_h, hg_gain) * jax.nn.silu(hg.astype(jnp.float32))

    log_a = jax.nn.log_sigmoid((ga @ gla_w_gate_up).astype(jnp.float32)
                               + gla_b_gate.astype(jnp.float32)) / GLA_GATE_TAU
    o_g = chunk_gated_linear_attention(heads(gq, GLA_HEADS) * (GLA_KEY_DIM ** -0.5),
                                       heads(gk, GLA_HEADS), heads(gv, GLA_HEADS),
                                       heads(log_a, GLA_HEADS))
    gla = head_rms_norm(o_g, gla_gain) * jax.nn.silu(gg.astype(jnp.float32))

    mixed = jnp.concatenate([sb, hgrn.astype(u.dtype), gla.astype(u.dtype)], axis=-1)
    return mixed @ w_out


def setup_inputs(seed: int = 0) -> dict:
    key = jax.random.key(seed)
    ks = jax.random.split(key, 24)
    f32 = jnp.float32

    def w(k, shape, fan_in):
        return jax.random.normal(k, shape, f32) * (fan_in ** -0.5)

    def gain(k, shape):
        return 1.0 + 0.01 * jax.random.normal(k, shape, f32)

    return {
        "x": jax.random.normal(ks[0], (BATCH, SEQ, D_MODEL), f32),
        "p": jax.random.normal(ks[1], (DEPTH, BATCH, SEQ, D_PLE), f32),
        "ffn1_norm": gain(ks[2], (DEPTH, D_MODEL)),
        "ffn1_w_gate": w(ks[3], (DEPTH, D_MODEL, D_FF), D_MODEL),
        "ffn1_w_up": w(ks[4], (DEPTH, D_MODEL, D_FF), D_MODEL),
        "ffn1_w_down": w(ks[5], (DEPTH, D_FF, D_MODEL), D_FF),
        "mix_norm": gain(ks[6], (DEPTH, D_MODEL)),
        "w_in": w(ks[7], (DEPTH, D_MODEL, D_IN), D_MODEL),
        "w_out": w(ks[8], (DEPTH, D_MIX, D_MODEL), D_MIX),
        "hgrn_lb_logits": jax.random.normal(ks[9], (DEPTH, HG_WIDTH), f32),
        "hgrn_out_gain": gain(ks[10], (DEPTH, HG_WIDTH)),
        "gla_w_gate_up": w(ks[11], (DEPTH, GLA_GATE_RANK, GLA_HEADS * GLA_KEY_DIM), GLA_GATE_RANK),
        "gla_b_gate": 0.01 * jax.random.normal(ks[12], (DEPTH, GLA_HEADS * GLA_KEY_DIM), f32),
        "gla_out_gain": gain(ks[13], (DEPTH, GLA_WIDTH)),
        "ffn2_norm": gain(ks[14], (DEPTH, D_MODEL)),
        "ffn2_w_gate": w(ks[15], (DEPTH, D_MODEL, D_FF), D_MODEL),
        "ffn2_w_up": w(ks[16], (DEPTH, D_MODEL, D_FF), D_MODEL),
        "ffn2_w_down": w(ks[17], (DEPTH, D_FF, D_MODEL), D_FF),
        "ple_norm": gain(ks[18], (DEPTH, D_MODEL)),
        "ple_w_gate": w(ks[19], (DEPTH, D_MODEL, D_MODEL), D_MODEL),
        "ple_w_proj": w(ks[20], (DEPTH, D_PLE, D_MODEL), D_PLE),
        "final_norm": gain(ks[21], (D_MODEL,)),
    }


def reference(x, p, ffn1_norm, ffn1_w_gate, ffn1_w_up, ffn1_w_down, mix_norm, w_in, w_out,
              hgrn_lb_logits, hgrn_out_gain, gla_w_gate_up, gla_b_gate, gla_out_gain,
              ffn2_norm, ffn2_w_gate, ffn2_w_up, ffn2_w_down, ple_norm, ple_w_gate, ple_w_proj,
              final_norm):
    probs = jax.nn.softmax(hgrn_lb_logits.astype(jnp.float32), axis=0)
    lower_bounds = jnp.cumsum(probs, axis=0) - probs[0]
    h = x
    for i in range(DEPTH):
        h = h + 0.5 * swiglu(rms_norm(h, ffn1_norm[i]), ffn1_w_gate[i], ffn1_w_up[i], ffn1_w_down[i])
        lb = None if i == 0 else lower_bounds[i]
        h = h + token_mixing(rms_norm(h, mix_norm[i]), w_in[i], w_out[i], lb, hgrn_out_gain[i],
                             gla_w_gate_up[i], gla_b_gate[i], gla_out_gain[i])
        h = h + 0.5 * swiglu(rms_norm(h, ffn2_norm[i]), ffn2_w_gate[i], ffn2_w_up[i], ffn2_w_down[i])
        gate = jax.nn.sigmoid(rms_norm(h, ple_norm[i]) @ ple_w_gate[i])
        h = h + gate * (p[i] @ ple_w_proj[i])
    return rms_norm(h, final_norm)
```

```python
import functools

import jax
import jax.numpy as jnp
from jax import lax
from jax.experimental import pallas as pl
from jax.experimental.pallas import tpu as pltpu

F32 = jnp.float32
BF16 = jnp.bfloat16

EPS = 1e-6
CHUNK = 64
SB_HEAD_DIM = 64
SB_PAIR = 128
HG_HEADS = 4
GLA_HEADS = 4
GLA_GATE_TAU = 16.0
GLA_GATE_PAD = 128

V7X_VMEM_LIMIT_BYTES = 56 * 1024 * 1024
ROW_TILE = 512
FF_CHUNK = 256
SB_BLOCK = 256
SEQ_TILE = 512
INTRA_GROUP = 16


def _params(*sem):
    return pltpu.CompilerParams(dimension_semantics=sem, vmem_limit_bytes=V7X_VMEM_LIMIT_BYTES)


def _dot(a, b):
    return jnp.dot(a, b, preferred_element_type=F32)


def _dot_nt(a, b):
    return lax.dot_general(a, b, (((1,), (1,)), ((), ())), preferred_element_type=F32)


def _dot_tn(a, b):
    return lax.dot_general(a, b, (((0,), (0,)), ((), ())), preferred_element_type=F32)


def _split_bf16(x):
    hi = x.astype(BF16)
    lo = (x - hi.astype(F32)).astype(BF16)
    return hi, lo


def _dot_split(x, w):
    hi, lo = _split_bf16(x)
    return _dot(hi, w) + _dot(lo, w)


def _rms(x, g):
    ms = jnp.mean(x * x, axis=-1, keepdims=True)
    return x * lax.rsqrt(ms + EPS) * g


def _softplus(x):
    return jnp.maximum(x, 0.0) + jnp.log(1.0 + jnp.exp(-jnp.abs(x)))


def _silu(x):
    return x * jax.nn.sigmoid(x)


def _const_spec(shape):
    return pl.BlockSpec(shape, lambda *_: (0,) * len(shape), pipeline_mode=pl.Buffered(1))


def _ffn_kernel(h_ref, g_ref, wg_ref, wu_ref, wd_ref, o_ref, a_ref):
    h = h_ref[...]
    u = _rms(h, g_ref[...]).astype(BF16)
    for j in range(wg_ref.shape[1] // FF_CHUNK):
        sl = slice(j * FF_CHUNK, (j + 1) * FF_CHUNK)
        gate = _dot(u, wg_ref[:, sl])
        up = _dot(u, wu_ref[:, sl])
        a_ref[:, sl] = (_silu(gate) * up).astype(BF16)
    o_ref[...] = h + 0.5 * _dot(a_ref[...], wd_ref[...])


def _ffn(h, g, wg, wu, wd):
    m, d = h.shape
    f = wg.shape[1]
    tm = min(ROW_TILE, m)
    row = pl.BlockSpec((tm, d), lambda i: (i, 0))
    return pl.pallas_call(
        _ffn_kernel,
        grid=(m // tm,),
        in_specs=[row, _const_spec((1, d)), _const_spec((d, f)), _const_spec((d, f)), _const_spec((f, d))],
        out_specs=row,
        out_shape=jax.ShapeDtypeStruct((m, d), F32),
        scratch_shapes=[pltpu.VMEM((tm, f), BF16)],
        compiler_params=_params("parallel"),
        name="ffn",
    )(h, g, wg, wu, wd)


def _inproj_kernel(h_ref, g_ref, w1_ref, w2_ref, w3_ref, o1_ref, o2_ref, o3_ref):
    u = _rms(h_ref[...], g_ref[...]).astype(BF16)
    o1_ref[...] = _dot(u, w1_ref[...]).astype(BF16)
    o2_ref[...] = _dot(u, w2_ref[...])
    o3_ref[...] = _dot(u, w3_ref[...])


def _inproj(h, g, w1, w2, w3):
    m, d = h.shape
    tm = min(ROW_TILE, m)
    row = lambda n: pl.BlockSpec((tm, n), lambda i: (i, 0))
    n1, n2, n3 = w1.shape[1], w2.shape[1], w3.shape[1]
    return pl.pallas_call(
        _inproj_kernel,
        grid=(m // tm,),
        in_specs=[row(d), _const_spec((1, d)), _const_spec((d, n1)), _const_spec((d, n2)), _const_spec((d, n3))],
        out_specs=[row(n1), row(n2), row(n3)],
        out_shape=[jax.ShapeDtypeStruct((m, n1), BF16), jax.ShapeDtypeStruct((m, n2), F32),
                   jax.ShapeDtypeStruct((m, n3), F32)],
        compiler_params=_params("parallel"),
        name="inproj",
    )(h, g, w1, w2, w3)


def _outproj_kernel(h_ref, sb_ref, hg_ref, gl_ref, w1_ref, w2_ref, w3_ref, o_ref):
    o_ref[...] = (h_ref[...] + _dot(sb_ref[...], w1_ref[...]) + _dot(hg_ref[...], w2_ref[...])
                  + _dot(gl_ref[...], w3_ref[...]))


def _outproj(h, sb, hg, gl, w1, w2, w3):
    m, d = h.shape
    tm = min(ROW_TILE, m)
    row = lambda n: pl.BlockSpec((tm, n), lambda i: (i, 0))
    return pl.pallas_call(
        _outproj_kernel,
        grid=(m // tm,),
        in_specs=[row(d), row(sb.shape[1]), row(hg.shape[1]), row(gl.shape[1]),
                  _const_spec(w1.shape), _const_spec(w2.shape), _const_spec(w3.shape)],
        out_specs=row(d),
        out_shape=jax.ShapeDtypeStruct((m, d), F32),
        compiler_params=_params("parallel"),
        name="outproj",
    )(h, sb, hg, gl, w1, w2, w3)


def _ple_kernel(h_ref, p_ref, g_ref, wg_ref, wp_ref, fg_ref, o_ref, *, final):
    h = h_ref[...]
    u = _rms(h, g_ref[...]).astype(BF16)
    gate = jax.nn.sigmoid(_dot(u, wg_ref[...]))
    h = h + gate * _dot(p_ref[...].astype(BF16), wp_ref[...])
    if final:
        h = _rms(h, fg_ref[...])
    o_ref[...] = h


def _ple(h, p, g, wg, wp, fg, final):
    m, d = h.shape
    dp = p.shape[1]
    tm = min(ROW_TILE, m)
    row = lambda n: pl.BlockSpec((tm, n), lambda i: (i, 0))
    return pl.pallas_call(
        functools.partial(_ple_kernel, final=final),
        grid=(m // tm,),
        in_specs=[row(d), row(dp), _const_spec((1, d)), _const_spec((d, d)), _const_spec((dp, d)),
                  _const_spec((1, d))],
        out_specs=row(d),
        out_shape=jax.ShapeDtypeStruct((m, d), F32),
        compiler_params=_params("parallel"),
        name="ple",
    )(h, p, g, wg, wp, fg)


def _sb_kernel(q_ref, k_ref, v_ref, tri_ref, o_ref):
    tq = q_ref.shape[1]
    qi = pl.program_id(2)
    q = q_ref[0] * jnp.asarray(SB_HEAD_DIM ** -0.5, BF16)
    lane = lax.broadcasted_iota(jnp.int32, (1, SB_PAIR), 1)
    first = lane < SB_HEAD_DIM
    q_heads = (jnp.where(first, q, jnp.zeros_like(q)), jnp.where(first, jnp.zeros_like(q), q))
    tri = tri_ref[...]
    row = lax.broadcasted_iota(jnp.int32, (tq, tq), 0)
    col = lax.broadcasted_iota(jnp.int32, (tq, tq), 1)
    strict = col < row

    def block(kb, carry, diagonal):
        off = pl.multiple_of(kb * tq, tq)
        kblk = k_ref[0, pl.ds(off, tq), :]
        vblk = v_ref[0, pl.ds(off, tq), :]
        out = []
        for qh, (run, acc) in zip(q_heads, carry):
            z = _dot_nt(qh, kblk)
            sp = _softplus(z)
            if diagonal:
                sp = jnp.where(strict, sp, 0.0)
            cs = _dot_split(sp, tri)
            w = jnp.exp(z - cs + run)
            if diagonal:
                w = jnp.where(strict, w, 0.0)
            acc = acc + _dot(w.astype(BF16), vblk)
            out.append((run - cs[:, 0:1], acc))
        return tuple(out)

    zero = (jnp.zeros((tq, 1), F32), jnp.zeros((tq, SB_PAIR), F32))
    carry = block(qi, (zero, zero), True)
    carry = lax.fori_loop(0, qi, lambda j, c: block(qi - 1 - j, c, False), carry)
    o_ref[0] = jnp.where(first, carry[0][1], carry[1][1]).astype(o_ref.dtype)


def _sb_attention(qkv, width):
    b, t, _ = qkv.shape
    pairs = width // SB_PAIR
    tq = min(SB_BLOCK, t)
    idx = jnp.arange(tq)
    tri = (idx[:, None] >= idx[None, :]).astype(BF16)
    return pl.pallas_call(
        _sb_kernel,
        grid=(b, pairs, t // tq),
        in_specs=[pl.BlockSpec((1, tq, SB_PAIR), lambda bi, p, i: (bi, i, p)),
                  pl.BlockSpec((1, t, SB_PAIR), lambda bi, p, i: (bi, 0, pairs + p)),
                  pl.BlockSpec((1, t, SB_PAIR), lambda bi, p, i: (bi, 0, 2 * pairs + p)),
                  _const_spec((tq, tq))],
        out_specs=pl.BlockSpec((1, tq, SB_PAIR), lambda bi, p, i: (bi, i, p)),
        out_shape=jax.ShapeDtypeStruct((b, t, width), BF16),
        compiler_params=_params("parallel", "parallel", "arbitrary"),
        name="sb_attention",
    )(qkv, qkv, qkv, tri)


def _intra_layout():
    offsets, total = [], 0
    for s in range(CHUNK):
        offsets.append(total)
        total += CHUNK - (s // INTRA_GROUP) * INTRA_GROUP
    return offsets, total


_INTRA_OFFSETS, _INTRA_ROWS = _intra_layout()


def _gla_tile(q_s, k_s, v_s, g_s, o_s, st_ref, b_ref, e_ref, p_ref, heads):
    rows, lk = q_s.shape
    lv = v_s.shape[1]
    dk, dv = lk // heads, lv // heads
    r = lax.broadcasted_iota(jnp.int32, (CHUNK, CHUNK), 0)
    c = lax.broadcasted_iota(jnp.int32, (CHUNK, CHUNK), 1)
    tril = (c <= r).astype(BF16)
    head_of_k = lax.broadcasted_iota(jnp.int32, (lk, lv), 0) // dk
    head_of_v = lax.broadcasted_iota(jnp.int32, (lk, lv), 1) // dv
    spread = (head_of_k == head_of_v).astype(BF16)
    same_head = (lax.broadcasted_iota(jnp.int32, (lv, lk), 0) // dv
                 == lax.broadcasted_iota(jnp.int32, (lv, lk), 1) // dk)
    row_id = lax.broadcasted_iota(jnp.int32, (CHUNK, 1), 0)

    def chunk(ci, _):
        base = pl.multiple_of(ci * CHUNK, CHUNK)
        q = q_s[pl.ds(base, CHUNK), :]
        k = k_s[pl.ds(base, CHUNK), :]
        v = v_s[pl.ds(base, CHUNK), :]
        g = g_s[pl.ds(base, CHUNK), :]
        g_hi, g_lo = _split_bf16(g)
        b = _dot(tril, g_hi) + _dot(tril, g_lo)
        b_ref[...] = b
        b_last = b[CHUNK - 1:CHUNK, :]
        st = st_ref[...]
        o_inter = _dot_nt((q * jnp.exp(b)).astype(BF16), st.astype(BF16))
        k_dec = (k * jnp.exp(b_last - b)).astype(BF16)
        upd = _dot_tn(v.astype(BF16), k_dec)
        st_ref[...] = st * jnp.exp(b_last) + jnp.where(same_head, upd, 0.0)

        for s in range(CHUNK):
            r0 = (s // INTRA_GROUP) * INTRA_GROUP
            n = CHUNK - r0
            pair = jnp.exp(b[r0:, :] - b_ref[s:s + 1, :]) * (q[r0:, :] * k_s[pl.ds(base + s, 1), :])
            if s % INTRA_GROUP:
                pair = jnp.where(row_id[r0:, :] >= s, pair, 0.0)
            e_ref[_INTRA_OFFSETS[s]:_INTRA_OFFSETS[s] + n, :] = pair.astype(BF16)
        p_ref[...] = _dot(e_ref[...], spread)
        for rg in range(CHUNK // 8):
            acc = o_inter[8 * rg:8 * rg + 8, :]
            for s in range(8 * rg + 8):
                r0 = (s // INTRA_GROUP) * INTRA_GROUP
                at = _INTRA_OFFSETS[s] + 8 * rg - r0
                acc = acc + p_ref[at:at + 8, :] * v_s[pl.ds(base + s, 1), :]
            o_s[pl.ds(base + 8 * rg, 8), :] = acc
        return 0

    lax.fori_loop(0, rows // CHUNK, chunk, 0)


def _head_norm_gate(o, gain, gate, heads):
    lv = o.shape[1]
    dv = lv // heads
    same = (lax.broadcasted_iota(jnp.int32, (lv, lv), 0) // dv
            == lax.broadcasted_iota(jnp.int32, (lv, lv), 1) // dv).astype(BF16)
    ms = _dot_split(o * o, same) * (1.0 / dv)
    return o * lax.rsqrt(ms + EPS) * gain * _silu(gate)


def _hgrn_kernel(q_ref, f_ref, i_ref, gate_ref, logits_ref, gain_ref, o_ref,
                 q_s, k_s, v_s, g_s, o_s, st_ref, b_ref, e_ref, p_ref, *, layer):
    @pl.when(pl.program_id(1) == 0)
    def _():
        st_ref[...] = jnp.zeros_like(st_ref)

    fl = f_ref[0]
    if layer == 0:
        log_f = -_softplus(-fl)
    else:
        logits = logits_ref[...]
        depth = logits.shape[0]
        top = logits[0:1, :]
        for j in range(1, depth):
            top = jnp.maximum(top, logits[j:j + 1, :])
        ex = [jnp.exp(logits[j:j + 1, :] - top) for j in range(depth)]
        lb = sum(ex[1:layer + 1]) / sum(ex)
        log_f = jnp.log(lb + (1.0 - lb) * jax.nn.sigmoid(fl))
    q_s[...] = _silu(q_ref[0])
    k_s[...] = 1.0 - jnp.exp(log_f)
    v_s[...] = i_ref[0]
    g_s[...] = log_f
    _gla_tile(q_s, k_s, v_s, g_s, o_s, st_ref, b_ref, e_ref, p_ref, HG_HEADS)
    o_ref[0] = _head_norm_gate(o_s[...], gain_ref[...], gate_ref[0], HG_HEADS).astype(o_ref.dtype)


def _gla_kernel(q_ref, k_ref, v_ref, gate_ref, a_ref, wup_ref, bias_ref, gain_ref, o_ref,
                q_s, k_s, v_s, g_s, o_s, st_ref, b_ref, e_ref, p_ref):
    @pl.when(pl.program_id(1) == 0)
    def _():
        st_ref[...] = jnp.zeros_like(st_ref)

    dk =q_ref.shape[2] // GLA_HEADS
    pre = _dot(a_ref[0].astype(BF16), wup_ref[...]) + bias_ref[...]
    q_s[...] = q_ref[0] * (dk ** -0.5)
    k_s[...] = k_ref[0]
    v_s[...] = v_ref[0]
    g_s[...] = -_softplus(-pre) * (1.0 / GLA_GATE_TAU)
    _gla_tile(q_s, k_s, v_s, g_s, o_s, st_ref, b_ref, e_ref, p_ref, GLA_HEADS)
    o_ref[0] = _head_norm_gate(o_s[...], gain_ref[...], gate_ref[0], GLA_HEADS).astype(o_ref.dtype)


def _recurrent_scratch(tc, lk, lv):
    return [pltpu.VMEM((tc, lk), F32), pltpu.VMEM((tc, lk), F32), pltpu.VMEM((tc, lv), F32),
            pltpu.VMEM((tc, lk), F32), pltpu.VMEM((tc, lv), F32), pltpu.VMEM((lv, lk), F32),
            pltpu.VMEM((CHUNK, lk), F32), pltpu.VMEM((_INTRA_ROWS, lk), BF16),
            pltpu.VMEM((_INTRA_ROWS, lv), F32)]


def _hgrn(y, logits, gain, layer):
    b, t, w4 = y.shape
    w = w4 // 4
    tc = min(SEQ_TILE, t)
    col = lambda j: pl.BlockSpec((1, tc, w), lambda bi, i: (bi, i, j))
    return pl.pallas_call(
        functools.partial(_hgrn_kernel, layer=layer),
        grid=(b, t // tc),
        in_specs=[col(0), col(1), col(2), col(3), _const_spec(logits.shape), _const_spec((1, w))],
        out_specs=pl.BlockSpec((1, tc, w), lambda bi, i: (bi, i, 0)),
        out_shape=jax.ShapeDtypeStruct((b, t, w), BF16),
        scratch_shapes=_recurrent_scratch(tc, w, w),
        compiler_params=_params("parallel", "arbitrary"),
        name="hgrn2",
    )(y, y, y, y, logits, gain)


def _gla(y, wup, bias, gain, lk, lv):
    b, t, _ = y.shape
    tc = min(SEQ_TILE, t)
    blk = lambda n, j: pl.BlockSpec((1, tc, n), lambda bi, i: (bi, i, j))
    a_col = (2 * lk + 2 * lv) // GLA_GATE_PAD
    return pl.pallas_call(
        _gla_kernel,
        grid=(b, t // tc),
        in_specs=[blk(lk, 0), blk(lk, 1), blk(lv, (2 * lk) // lv), blk(lv, (2 * lk) // lv + 1),
                  blk(GLA_GATE_PAD, a_col), _const_spec(wup.shape), _const_spec((1, lk)),
                  _const_spec((1, lv))],
        out_specs=pl.BlockSpec((1, tc, lv), lambda bi, i: (bi, i, 0)),
        out_shape=jax.ShapeDtypeStruct((b, t, lv), BF16),
        scratch_shapes=_recurrent_scratch(tc, lk, lv),
        compiler_params=_params("parallel", "arbitrary"),
        name="gla",
    )(y, y, y, y, y, wup, bias, gain)


def kernel(x, p, ffn1_norm, ffn1_w_gate, ffn1_w_up, ffn1_w_down, mix_norm, w_in, w_out, hgrn_lb_logits, hgrn_out_gain, gla_w_gate_up, gla_b_gate, gla_out_gain, ffn2_norm, ffn2_w_gate, ffn2_w_up, ffn2_w_down, ple_norm, ple_w_gate, ple_w_proj, final_norm):
    bsz, t, d = x.shape
    depth = p.shape[0]
    m = bsz * t
    sb_w = d // 2
    hg_w = hgrn_out_gain.shape[1]
    gla_v = gla_out_gain.shape[1]
    gla_k = gla_w_gate_up.shape[2]
    rank = gla_w_gate_up.shape[1]
    bf = lambda a: a.astype(BF16)
    row = lambda a: a.reshape(1, -1)

    h = x.reshape(m, d)
    for i in range(depth):
        h = _ffn(h, row(ffn1_norm[i]), bf(ffn1_w_gate[i]), bf(ffn1_w_up[i]), bf(ffn1_w_down[i]))

        wi = w_in[i]
        c_sb, c_hg = 3 * sb_w, 3 * sb_w + 4 * hg_w
        c_v = c_hg + 2 * gla_k
        c_a = c_v + gla_v
        c_g = c_a + rank
        w_gla = jnp.concatenate(
            [wi[:, c_hg:c_a], wi[:, c_g:c_g + gla_v], wi[:, c_a:c_g],
             jnp.zeros((d, GLA_GATE_PAD - rank), wi.dtype)], axis=1)
        y_sb, y_hg, y_gla = _inproj(h, row(mix_norm[i]), bf(wi[:, :c_sb]), bf(wi[:, c_sb:c_hg]), bf(w_gla))

        sb = _sb_attention(y_sb.reshape(bsz, t, -1), sb_w)
        hg = _hgrn(y_hg.reshape(bsz, t, -1), hgrn_lb_logits, row(hgrn_out_gain[i]), i)
        wup = jnp.concatenate([gla_w_gate_up[i], jnp.zeros((GLA_GATE_PAD - rank, gla_k), F32)], axis=0)
        gl = _gla(y_gla.reshape(bsz, t, -1), bf(wup), row(gla_b_gate[i]), row(gla_out_gain[i]), gla_k, gla_v)

        wo = bf(w_out[i])
        h = _outproj(h, sb.reshape(m, -1), hg.reshape(m, -1), gl.reshape(m, -1),
                     wo[:sb_w], wo[sb_w:sb_w + hg_w], wo[sb_w + hg_w:])
        h = _ffn(h, row(ffn2_norm[i]), bf(ffn2_w_gate[i]), bf(ffn2_w_up[i]), bf(ffn2_w_down[i]))
        h = _ple(h, p[i].reshape(m, -1), row(ple_norm[i]), bf(ple_w_gate[i]), bf(ple_w_proj[i]),
                 row(final_norm), i == depth - 1)
    return h.reshape(bsz, t, d)
```

```python
import functools

import jax
import jax.numpy as jnp
from jax import lax
from jax.experimental import pallas as pl
from jax.experimental.pallas import tpu as pltpu

F32 = jnp.float32
BF16 = jnp.bfloat16

EPS = 1e-6
CHUNK = 64
SB_HEAD_DIM = 64
SB_PAIR = 128
HG_HEADS = 4
GLA_HEADS = 4
GLA_GATE_TAU = 16.0
GLA_GATE_PAD = 128

V7X_VMEM_LIMIT_BYTES = 56 * 1024 * 1024
ROW_TILE = 512
FF_CHUNK = 256
SB_BLOCK = 128
SB_NORM_ROWS = 512
SB_BOUND_SLACK = 1.02
SB_ZERO_LOG = -105.0
SEQ_TILE = 512
INTRA_GROUP = 16


def _params(*sem):
    return pltpu.CompilerParams(dimension_semantics=sem, vmem_limit_bytes=V7X_VMEM_LIMIT_BYTES)


def _dot(a, b):
    return jnp.dot(a, b, preferred_element_type=F32)


def _dot_nt(a, b):
    return lax.dot_general(a, b, (((1,), (1,)), ((), ())), preferred_element_type=F32)


def _dot_tn(a, b):
    return lax.dot_general(a, b, (((0,), (0,)), ((), ())), preferred_element_type=F32)


def _split_bf16(x):
    hi = x.astype(BF16)
    lo = (x - hi.astype(F32)).astype(BF16)
    return hi, lo


def _dot_split(x, w):
    hi, lo = _split_bf16(x)
    return _dot(hi, w) + _dot(lo, w)


def _rms(x, g):
    ms = jnp.mean(x * x, axis=-1, keepdims=True)
    return x * lax.rsqrt(ms + EPS) * g


def _softplus(x):
    return jnp.maximum(x, 0.0) + jnp.log(1.0 + jnp.exp(-jnp.abs(x)))


def _silu(x):
    return x * jax.nn.sigmoid(x)


def _const_spec(shape):
    return pl.BlockSpec(shape, lambda *_: (0,) * len(shape), pipeline_mode=pl.Buffered(1))


def _ffn_kernel(h_ref, g_ref, wg_ref, wu_ref, wd_ref, o_ref, a_ref):
    h = h_ref[...]
    u = _rms(h, g_ref[...]).astype(BF16)
    for j in range(wg_ref.shape[1] // FF_CHUNK):
        sl = slice(j * FF_CHUNK, (j + 1) * FF_CHUNK)
        gate = _dot(u, wg_ref[:, sl])
        up = _dot(u, wu_ref[:, sl])
        a_ref[:, sl] = (_silu(gate) * up).astype(BF16)
    o_ref[...] = h + 0.5 * _dot(a_ref[...], wd_ref[...])


def _ffn(h, g, wg, wu, wd):
    m, d = h.shape
    f = wg.shape[1]
    tm = min(ROW_TILE, m)
    row = pl.BlockSpec((tm, d), lambda i: (i, 0))
    return pl.pallas_call(
        _ffn_kernel,
        grid=(m // tm,),
        in_specs=[row, _const_spec((1, d)), _const_spec((d, f)), _const_spec((d, f)), _const_spec((f, d))],
        out_specs=row,
        out_shape=jax.ShapeDtypeStruct((m, d), F32),
        scratch_shapes=[pltpu.VMEM((tm, f), BF16)],
        compiler_params=_params("parallel"),
        name="ffn",
    )(h, g, wg, wu, wd)


def _inproj_kernel(h_ref, g_ref, w1_ref, w2_ref, w3_ref, o1_ref, o2_ref, o3_ref):
    u = _rms(h_ref[...], g_ref[...]).astype(BF16)
    o1_ref[...] = _dot(u, w1_ref[...]).astype(BF16)
    o2_ref[...] = _dot(u, w2_ref[...])
    o3_ref[...] = _dot(u, w3_ref[...])


def _inproj(h, g, w1, w2, w3):
    m, d = h.shape
    tm = min(ROW_TILE, m)
    row = lambda n: pl.BlockSpec((tm, n), lambda i: (i, 0))
    n1, n2, n3 = w1.shape[1], w2.shape[1], w3.shape[1]
    return pl.pallas_call(
        _inproj_kernel,
        grid=(m // tm,),
        in_specs=[row(d), _const_spec((1, d)), _const_spec((d, n1)), _const_spec((d, n2)), _const_spec((d, n3))],
        out_specs=[row(n1), row(n2), row(n3)],
        out_shape=[jax.ShapeDtypeStruct((m, n1), BF16), jax.ShapeDtypeStruct((m, n2), F32),
                   jax.ShapeDtypeStruct((m, n3), F32)],
        compiler_params=_params("parallel"),
        name="inproj",
    )(h, g, w1, w2, w3)


def _outproj_kernel(h_ref, sb_ref, hg_ref, gl_ref, w1_ref, w2_ref, w3_ref, o_ref):
    o_ref[...] = (h_ref[...] + _dot(sb_ref[...], w1_ref[...]) + _dot(hg_ref[...], w2_ref[...])
                  + _dot(gl_ref[...], w3_ref[...]))


def _outproj(h, sb, hg, gl, w1, w2, w3):
    m, d = h.shape
    tm = min(ROW_TILE, m)
    row = lambda n: pl.BlockSpec((tm, n), lambda i: (i, 0))
    return pl.pallas_call(
        _outproj_kernel,
        grid=(m // tm,),
        in_specs=[row(d), row(sb.shape[1]), row(hg.shape[1]), row(gl.shape[1]),
                  _const_spec(w1.shape), _const_spec(w2.shape), _const_spec(w3.shape)],
        out_specs=row(d),
        out_shape=jax.ShapeDtypeStruct((m, d), F32),
        compiler_params=_params("parallel"),
        name="outproj",
    )(h, sb, hg, gl, w1, w2, w3)


def _ple_kernel(h_ref, p_ref, g_ref, wg_ref, wp_ref, fg_ref, o_ref, *, final):
    h = h_ref[...]
    u = _rms(h, g_ref[...]).astype(BF16)
    gate = jax.nn.sigmoid(_dot(u, wg_ref[...]))
    h = h + gate * _dot(p_ref[...].astype(BF16), wp_ref[...])
    if final:
        h = _rms(h, fg_ref[...])
    o_ref[...] = h


def _ple(h, p, g, wg, wp, fg, final):
    m, d = h.shape
    dp = p.shape[1]
    tm = min(ROW_TILE, m)
    row = lambda n: pl.BlockSpec((tm, n), lambda i: (i, 0))
    return pl.pallas_call(
        functools.partial(_ple_kernel, final=final),
        grid=(m // tm,),
        in_specs=[row(d), row(dp), _const_spec((1, d)), _const_spec((d, d)), _const_spec((dp, d)),
                  _const_spec((1, d))],
        out_specs=row(d),
        out_shape=jax.ShapeDtypeStruct((m, d), F32),
        compiler_params=_params("parallel"),
        name="ple",
    )(h, p, g, wg, wp, fg)


def _sb_kernel(q_ref, k_ref, v_ref, tri_ref, o_ref, q2_ref, zb_ref, run_ref, acc_ref, kmax_ref):
    tq, width = q_ref.shape[1], q_ref.shape[2]
    t = k_ref.shape[1]
    pairs = width // SB_PAIR
    qi = pl.program_id(1)
    lane = lax.broadcasted_iota(jnp.int32, (1, SB_PAIR), 1)
    first = lane < SB_HEAD_DIM
    ones = jnp.ones((SB_PAIR, SB_PAIR), BF16)
    pair_lanes = lambda p: slice(p * SB_PAIR, (p + 1) * SB_PAIR)
    pair_rows = lambda p: slice(2 * p * tq, (2 * p + 2) * tq)

    def head_split(x):
        zero = jnp.zeros_like(x)
        return jnp.where(first, x, zero), jnp.where(first, zero, x)

    def sq_norm(x):
        xf = x.astype(F32)
        return _dot((xf * xf).astype(BF16), ones)

    @pl.when(qi == 0)
    def _():
        for p in range(pairs):
            def body(i, m, p=p):
                rows = pl.ds(pl.multiple_of(i * SB_NORM_ROWS, SB_NORM_ROWS), SB_NORM_ROWS)
                ka, kb_ = head_split(k_ref[0, rows, pair_lanes(p)])
                return (jnp.maximum(m[0], jnp.max(sq_norm(ka), axis=0, keepdims=True)),
                        jnp.maximum(m[1], jnp.max(sq_norm(kb_), axis=0, keepdims=True)))
            zero = jnp.zeros((1, SB_PAIR), F32)
            ma, mb = lax.fori_loop(0, t // SB_NORM_ROWS, body, (zero, zero))
            kmax_ref[2 * p:2 * p + 1, :] = ma
            kmax_ref[2 * p + 1:2 * p + 2, :] = mb

    for p in range(pairs):
        q = q_ref[0, :, pair_lanes(p)] * jnp.asarray(SB_HEAD_DIM ** -0.5, BF16)
        for a, qh in enumerate(head_split(q)):
            h = 2 * p + a
            q2_ref[h * tq:(h + 1) * tq, :] = qh
            zb_ref[h * tq:(h + 1) * tq, :] = jnp.sqrt(sq_norm(qh) * kmax_ref[h:h + 1, :]) * SB_BOUND_SLACK

    row = lax.broadcasted_iota(jnp.int32, (2 * pairs * tq, tq), 0)
    col = lax.broadcasted_iota(jnp.int32, (2 * pairs * tq, tq), 1)
    strict = col < row % tq

    def step(kb, diagonal):
        off = pl.multiple_of(kb * tq, tq)
        z = jnp.concatenate(
            [_dot_nt(q2_ref[pair_rows(p), :], k_ref[0, pl.ds(off, tq), pair_lanes(p)]) for p in range(pairs)],
            axis=0)
        sp = _softplus(z)
        if diagonal:
            sp = jnp.where(strict, sp, 0.0)
        hi, lo = _split_bf16(sp)
        ct = _dot(jnp.concatenate([hi, lo], axis=1), tri_ref[...])
        if diagonal:
            w = jnp.where(strict, jnp.exp(z - ct[:, :tq]), 0.0).astype(BF16)
        else:
            w = jnp.exp(z - ct[:, :tq] + run_ref[...]).astype(BF16)
        for p in range(pairs):
            pv = _dot(w[pair_rows(p), :], v_ref[0, pl.ds(off, tq), pair_lanes(p)])
            if diagonal:
                acc_ref[pair_rows(p), :] = pv
            else:
                acc_ref[pair_rows(p), :] += pv
        if diagonal:
            run_ref[...] = -ct[:, tq:]
        else:
            run_ref[...] -= ct[:, tq:]
        return (jnp.max(run_ref[...] + zb_ref[...]) > SB_ZERO_LOG).astype(jnp.int32)

    def body(c):
        return c[0] + 1, step(qi - 1 - c[0], False)

    lax.while_loop(lambda c: jnp.logical_and(c[0] < qi, c[1] > 0), body, (jnp.int32(0), step(qi, True)))
    for p in range(pairs):
        o_ref[0, :, pair_lanes(p)] = jnp.where(
            first, acc_ref[2 * p * tq:(2 * p + 1) * tq, :], acc_ref[(2 * p + 1) * tq:(2 * p + 2) * tq, :]
        ).astype(o_ref.dtype)


def _sb_attention(qkv, width):
    b, t, _ = qkv.shape
    heads = width // SB_HEAD_DIM
    tq = min(SB_BLOCK, t)
    j = jnp.arange(2 * tq)[:, None] % tq
    s = jnp.arange(2 * tq)[None, :]
    tri = jnp.logical_or(s >= tq, j >= s).astype(BF16)
    stacked = lambda dt: pltpu.VMEM((heads * tq, SB_PAIR), dt)
    return pl.pallas_call(
        _sb_kernel,
        grid=(b, t // tq),
        in_specs=[pl.BlockSpec((1, tq, width), lambda bi, i: (bi, i, 0)),
                  pl.BlockSpec((1, t, width), lambda bi, i: (bi, 0, 1), pipeline_mode=pl.Buffered(1)),
                  pl.BlockSpec((1, t, width), lambda bi, i: (bi, 0, 2), pipeline_mode=pl.Buffered(1)),
                  _const_spec((2 * tq, 2 * tq))],
        out_specs=pl.BlockSpec((1, tq, width), lambda bi, i: (bi, i, 0)),
        out_shape=jax.ShapeDtypeStruct((b, t, width), BF16),
        scratch_shapes=[stacked(BF16), stacked(F32), stacked(F32), stacked(F32),
                        pltpu.VMEM((heads, SB_PAIR), F32)],
        compiler_params=_params("parallel", "arbitrary"),
        name="sb_attention",
    )(qkv, qkv, qkv, tri)


def _intra_layout():
    offsets, total = [], 0
    for s in range(CHUNK):
        offsets.append(total)
        total += CHUNK - (s // INTRA_GROUP) * INTRA_GROUP
    return offsets, total


_INTRA_OFFSETS, _INTRA_ROWS = _intra_layout()


def _gla_tile(q_s, k_s, v_s, g_s, o_s, st_ref, b_ref, e_ref, p_ref, heads):
    rows, lk = q_s.shape
    lv = v_s.shape[1]
    dk, dv = lk // heads, lv // heads
    r = lax.broadcasted_iota(jnp.int32, (CHUNK, CHUNK), 0)
    c = lax.broadcasted_iota(jnp.int32, (CHUNK, CHUNK), 1)
    tril = (c <= r).astype(BF16)
    head_of_k = lax.broadcasted_iota(jnp.int32, (lk, lv), 0) // dk
    head_of_v = lax.broadcasted_iota(jnp.int32, (lk, lv), 1) // dv
    spread = (head_of_k == head_of_v).astype(BF16)
    same_head = (lax.broadcasted_iota(jnp.int32, (lv, lk), 0) // dv
                 == lax.broadcasted_iota(jnp.int32, (lv, lk), 1) // dk)
    row_id = lax.broadcasted_iota(jnp.int32, (CHUNK, 1), 0)

    def chunk(ci, _):
        base = pl.multiple_of(ci * CHUNK, CHUNK)
        q = q_s[pl.ds(base, CHUNK), :]
        k = k_s[pl.ds(base, CHUNK), :]
        v = v_s[pl.ds(base, CHUNK), :]
        g = g_s[pl.ds(base, CHUNK), :]
        g_hi, g_lo = _split_bf16(g)
        b = _dot(tril, g_hi) + _dot(tril, g_lo)
        b_ref[...] = b
        b_last = b[CHUNK - 1:CHUNK, :]
        st = st_ref[...]
        o_inter = _dot_nt((q * jnp.exp(b)).astype(BF16), st.astype(BF16))
        k_dec = (k * jnp.exp(b_last - b)).astype(BF16)
        upd = _dot_tn(v.astype(BF16), k_dec)
        st_ref[...] = st * jnp.exp(b_last) + jnp.where(same_head, upd, 0.0)

        for s in range(CHUNK):
            r0 = (s // INTRA_GROUP) * INTRA_GROUP
            n = CHUNK - r0
            pair = jnp.exp(b[r0:, :] - b_ref[s:s + 1, :]) * (q[r0:, :] * k_s[pl.ds(base + s, 1), :])
            if s % INTRA_GROUP:
                pair = jnp.where(row_id[r0:, :] >= s, pair, 0.0)
            e_ref[_INTRA_OFFSETS[s]:_INTRA_OFFSETS[s] + n, :] = pair.astype(BF16)
        p_ref[...] = _dot(e_ref[...], spread)
        for rg in range(CHUNK // 8):
            acc = o_inter[8 * rg:8 * rg + 8, :]
            for s in range(8 * rg + 8):
                r0 = (s // INTRA_GROUP) * INTRA_GROUP
                at = _INTRA_OFFSETS[s] + 8 * rg - r0
                acc = acc + p_ref[at:at + 8, :] * v_s[pl.ds(base + s, 1), :]
            o_s[pl.ds(base + 8 * rg, 8), :] = acc
        return 0

    lax.fori_loop(0, rows // CHUNK, chunk, 0)


def _head_norm_gate(o, gain, gate, heads):
    lv = o.shape[1]
    dv = lv // heads
    same = (lax.broadcasted_iota(jnp.int32, (lv, lv), 0) // dv
            == lax.broadcasted_iota(jnp.int32, (lv, lv), 1) // dv).astype(BF16)
    ms = _dot_split(o * o, same) * (1.0 / dv)
    return o * lax.rsqrt(ms + EPS) * gain * _silu(gate)


def _hgrn_kernel(q_ref, f_ref, i_ref, gate_ref, logits_ref, gain_ref, o_ref,
                 q_s, k_s, v_s, g_s, o_s, st_ref, b_ref, e_ref, p_ref, *, layer):
    @pl.when(pl.program_id(1) == 0)
    def _():
        st_ref[...] = jnp.zeros_like(st_ref)

    fl = f_ref[0]
    if layer == 0:
        log_f = -_softplus(-fl)
    else:
        logits = logits_ref[...]
        depth = logits.shape[0]
        top = logits[0:1, :]
        for j in range(1, depth):
            top = jnp.maximum(top, logits[j:j + 1, :])
        ex = [jnp.exp(logits[j:j + 1, :] - top) for j in range(depth)]
        lb = sum(ex[1:layer + 1]) / sum(ex)
        log_f = jnp.log(lb + (1.0 - lb) * jax.nn.sigmoid(fl))
    q_s[...] = _silu(q_ref[0])
    k_s[...] = 1.0 - jnp.exp(log_f)
    v_s[...] = i_ref[0]
    g_s[...] = log_f
    _gla_tile(q_s, k_s, v_s, g_s, o_s, st_ref, b_ref, e_ref, p_ref, HG_HEADS)
    o_ref[0] = _head_norm_gate(o_s[...], gain_ref[...], gate_ref[0], HG_HEADS).astype(o_ref.dtype)


def _gla_kernel(q_ref, k_ref, v_ref, gate_ref, a_ref, wup_ref, bias_ref, gain_ref, o_ref,
                q_s, k_s, v_s, g_s, o_s, st_ref, b_ref, e_ref, p_ref):
    @pl.when(pl.program_id(1) == 0)
    def _():
        st_ref[...] = jnp.zeros_like(st_ref)

    dk =q_ref.shape[2] // GLA_HEADS
    pre = _dot(a_ref[0].astype(BF16), wup_ref[...]) + bias_ref[...]
    q_s[...] = q_ref[0] * (dk ** -0.5)
    k_s[...] = k_ref[0]
    v_s[...] = v_ref[0]
    g_s[...] = -_softplus(-pre) * (1.0 / GLA_GATE_TAU)
    _gla_tile(q_s, k_s, v_s, g_s, o_s, st_ref, b_ref, e_ref, p_ref, GLA_HEADS)
    o_ref[0] = _head_norm_gate(o_s[...], gain_ref[...], gate_ref[0], GLA_HEADS).astype(o_ref.dtype)


def _recurrent_scratch(tc, lk, lv):
    return [pltpu.VMEM((tc, lk), F32), pltpu.VMEM((tc, lk), F32), pltpu.VMEM((tc, lv), F32),
            pltpu.VMEM((tc, lk), F32), pltpu.VMEM((tc, lv), F32), pltpu.VMEM((lv, lk), F32),
            pltpu.VMEM((CHUNK, lk), F32), pltpu.VMEM((_INTRA_ROWS, lk), BF16),
            pltpu.VMEM((_INTRA_ROWS, lv), F32)]


def _hgrn(y, logits, gain, layer):
    b, t, w4 = y.shape
    w = w4 // 4
    tc = min(SEQ_TILE, t)
    col = lambda j: pl.BlockSpec((1, tc, w), lambda bi, i: (bi, i, j))
    return pl.pallas_call(
        functools.partial(_hgrn_kernel, layer=layer),
        grid=(b, t // tc),
        in_specs=[col(0), col(1), col(2), col(3), _const_spec(logits.shape), _const_spec((1, w))],
        out_specs=pl.BlockSpec((1, tc, w), lambda bi, i: (bi, i, 0)),
        out_shape=jax.ShapeDtypeStruct((b, t, w), BF16),
        scratch_shapes=_recurrent_scratch(tc, w, w),
        compiler_params=_params("parallel", "arbitrary"),
        name="hgrn2",
    )(y, y, y, y, logits, gain)


def _gla(y, wup, bias, gain, lk, lv):
    b, t, _ = y.shape
    tc = min(SEQ_TILE, t)
    blk = lambda n, j: pl.BlockSpec((1, tc, n), lambda bi, i: (bi, i, j))
    a_col = (2 * lk + 2 * lv) // GLA_GATE_PAD
    return pl.pallas_call(
        _gla_kernel,
        grid=(b, t // tc),
        in_specs=[blk(lk, 0), blk(lk, 1), blk(lv, (2 * lk) // lv), blk(lv, (2 * lk) // lv + 1),
                  blk(GLA_GATE_PAD, a_col), _const_spec(wup.shape), _const_spec((1, lk)),
                  _const_spec((1, lv))],
        out_specs=pl.BlockSpec((1, tc, lv), lambda bi, i: (bi, i, 0)),
        out_shape=jax.ShapeDtypeStruct((b, t, lv), BF16),
        scratch_shapes=_recurrent_scratch(tc, lk, lv),
        compiler_params=_params("parallel", "arbitrary"),
        name="gla",
    )(y, y, y, y, y, wup, bias, gain)


def kernel(x, p, ffn1_norm, ffn1_w_gate, ffn1_w_up, ffn1_w_down, mix_norm, w_in, w_out, hgrn_lb_logits, hgrn_out_gain, gla_w_gate_up, gla_b_gate, gla_out_gain, ffn2_norm, ffn2_w_gate, ffn2_w_up, ffn2_w_down, ple_norm, ple_w_gate, ple_w_proj, final_norm):
    bsz, t, d = x.shape
    depth = p.shape[0]
    m = bsz * t
    sb_w = d // 2
    hg_w = hgrn_out_gain.shape[1]
    gla_v = gla_out_gain.shape[1]
    gla_k = gla_w_gate_up.shape[2]
    rank = gla_w_gate_up.shape[1]
    bf = lambda a: a.astype(BF16)
    row = lambda a: a.reshape(1, -1)

    h = x.reshape(m, d)
    for i in range(depth):
        h = _ffn(h, row(ffn1_norm[i]), bf(ffn1_w_gate[i]), bf(ffn1_w_up[i]), bf(ffn1_w_down[i]))

        wi = w_in[i]
        c_sb, c_hg = 3 * sb_w, 3 * sb_w + 4 * hg_w
        c_v = c_hg + 2 * gla_k
        c_a = c_v + gla_v
        c_g = c_a + rank
        w_gla = jnp.concatenate(
            [wi[:, c_hg:c_a], wi[:, c_g:c_g + gla_v], wi[:, c_a:c_g],
             jnp.zeros((d, GLA_GATE_PAD - rank), wi.dtype)], axis=1)
        y_sb, y_hg, y_gla = _inproj(h, row(mix_norm[i]), bf(wi[:, :c_sb]), bf(wi[:, c_sb:c_hg]), bf(w_gla))

        sb = _sb_attention(y_sb.reshape(bsz, t, -1), sb_w)
        hg = _hgrn(y_hg.reshape(bsz, t, -1), hgrn_lb_logits, row(hgrn_out_gain[i]), i)
        wup = jnp.concatenate([gla_w_gate_up[i], jnp.zeros((GLA_GATE_PAD - rank, gla_k), F32)], axis=0)
        gl = _gla(y_gla.reshape(bsz, t, -1), bf(wup), row(gla_b_gate[i]), row(gla_out_gain[i]), gla_k, gla_v)

        wo = bf(w_out[i])
        h = _outproj(h, sb.reshape(m, -1), hg.reshape(m, -1), gl.reshape(m, -1),
                     wo[:sb_w], wo[sb_w:sb_w + hg_w], wo[sb_w + hg_w:])
        h = _ffn(h, row(ffn2_norm[i]), bf(ffn2_w_gate[i]), bf(ffn2_w_up[i]), bf(ffn2_w_down[i]))
        h = _ple(h, p[i].reshape(m, -1), row(ple_norm[i]), bf(ple_w_gate[i]), bf(ple_w_proj[i]),
                 row(final_norm), i == depth - 1)
    return h.reshape(bsz, t, d)
```

```python
import functools

import jax
import jax.numpy as jnp
from jax import lax
from jax.experimental import pallas as pl
from jax.experimental.pallas import tpu as pltpu

F32 = jnp.float32
BF16 = jnp.bfloat16

EPS = 1e-6
CHUNK = 32
CHUNK_UNROLL = 4
SB_HEAD_DIM = 64
SB_PAIR = 128
HG_HEADS = 4
GLA_HEADS = 4
GLA_GATE_TAU = 16.0
GLA_GATE_PAD = 128

V7X_VMEM_LIMIT_BYTES = 56 * 1024 * 1024
ROW_TILE = 512
FF_CHUNK = 256
SB_BLOCK = 128
SB_NORM_ROWS = 512
SB_BOUND_SLACK = 1.02
SB_ZERO_LOG = -105.0
SEQ_TILE = 512
INTRA_GROUP = 16


def _params(*sem):
    return pltpu.CompilerParams(dimension_semantics=sem, vmem_limit_bytes=V7X_VMEM_LIMIT_BYTES)


def _dot(a, b):
    return jnp.dot(a, b, preferred_element_type=F32)


def _dot_nt(a, b):
    return lax.dot_general(a, b, (((1,), (1,)), ((), ())), preferred_element_type=F32)


def _dot_tn(a, b):
    return lax.dot_general(a, b, (((0,), (0,)), ((), ())), preferred_element_type=F32)


def _split_bf16(x):
    hi = x.astype(BF16)
    lo = (x - hi.astype(F32)).astype(BF16)
    return hi, lo


def _dot_split(x, w):
    hi, lo = _split_bf16(x)
    return _dot(hi, w) + _dot(lo, w)


def _rms(x, g):
    ms = jnp.mean(x * x, axis=-1, keepdims=True)
    return x * lax.rsqrt(ms + EPS) * g


def _softplus(x):
    return jnp.maximum(x, 0.0) + jnp.log(1.0 + jnp.exp(-jnp.abs(x)))


def _silu(x):
    return x * jax.nn.sigmoid(x)


def _const_spec(shape):
    return pl.BlockSpec(shape, lambda *_: (0,) * len(shape), pipeline_mode=pl.Buffered(1))


def _ffn_kernel(h_ref, g_ref, wg_ref, wu_ref, wd_ref, o_ref, a_ref):
    h = h_ref[...]
    u = _rms(h, g_ref[...]).astype(BF16)
    for j in range(wg_ref.shape[1] // FF_CHUNK):
        sl = slice(j * FF_CHUNK, (j + 1) * FF_CHUNK)
        gate = _dot(u, wg_ref[:, sl])
        up = _dot(u, wu_ref[:, sl])
        a_ref[:, sl] = (_silu(gate) * up).astype(BF16)
    o_ref[...] = h + 0.5 * _dot(a_ref[...], wd_ref[...])


def _ffn(h, g, wg, wu, wd):
    m, d = h.shape
    f = wg.shape[1]
    tm = min(ROW_TILE, m)
    row = pl.BlockSpec((tm, d), lambda i: (i, 0))
    return pl.pallas_call(
        _ffn_kernel,
        grid=(m // tm,),
        in_specs=[row, _const_spec((1, d)), _const_spec((d, f)), _const_spec((d, f)), _const_spec((f, d))],
        out_specs=row,
        out_shape=jax.ShapeDtypeStruct((m, d), F32),
        scratch_shapes=[pltpu.VMEM((tm, f), BF16)],
        compiler_params=_params("parallel"),
        name="ffn",
    )(h, g, wg, wu, wd)


def _inproj_kernel(h_ref, g_ref, w1_ref, w2_ref, w3_ref, o1_ref, o2_ref, o3_ref):
    u = _rms(h_ref[...], g_ref[...]).astype(BF16)
    o1_ref[...] = _dot(u, w1_ref[...]).astype(BF16)
    o2_ref[...] = _dot(u, w2_ref[...])
    o3_ref[...] = _dot(u, w3_ref[...])


def _inproj(h, g, w1, w2, w3):
    m, d = h.shape
    tm = min(ROW_TILE, m)
    row = lambda n: pl.BlockSpec((tm, n), lambda i: (i, 0))
    n1, n2, n3 = w1.shape[1], w2.shape[1], w3.shape[1]
    return pl.pallas_call(
        _inproj_kernel,
        grid=(m // tm,),
        in_specs=[row(d), _const_spec((1, d)), _const_spec((d, n1)), _const_spec((d, n2)), _const_spec((d, n3))],
        out_specs=[row(n1), row(n2), row(n3)],
        out_shape=[jax.ShapeDtypeStruct((m, n1), BF16), jax.ShapeDtypeStruct((m, n2), F32),
                   jax.ShapeDtypeStruct((m, n3), F32)],
        compiler_params=_params("parallel"),
        name="inproj",
    )(h, g, w1, w2, w3)


def _outproj_kernel(h_ref, sb_ref, hg_ref, gl_ref, w1_ref, w2_ref, w3_ref, o_ref):
    o_ref[...] = (h_ref[...] + _dot(sb_ref[...], w1_ref[...]) + _dot(hg_ref[...], w2_ref[...])
                  + _dot(gl_ref[...], w3_ref[...]))


def _outproj(h, sb, hg, gl, w1, w2, w3):
    m, d = h.shape
    tm = min(ROW_TILE, m)
    row = lambda n: pl.BlockSpec((tm, n), lambda i: (i, 0))
    return pl.pallas_call(
        _outproj_kernel,
        grid=(m // tm,),
        in_specs=[row(d), row(sb.shape[1]), row(hg.shape[1]), row(gl.shape[1]),
                  _const_spec(w1.shape), _const_spec(w2.shape), _const_spec(w3.shape)],
        out_specs=row(d),
        out_shape=jax.ShapeDtypeStruct((m, d), F32),
        compiler_params=_params("parallel"),
        name="outproj",
    )(h, sb, hg, gl, w1, w2, w3)


def _ple_kernel(h_ref, p_ref, g_ref, wg_ref, wp_ref, fg_ref, o_ref, *, final):
    h = h_ref[...]
    u = _rms(h, g_ref[...]).astype(BF16)
    gate = jax.nn.sigmoid(_dot(u, wg_ref[...]))
    h = h + gate * _dot(p_ref[...].astype(BF16), wp_ref[...])
    if final:
        h = _rms(h, fg_ref[...])
    o_ref[...] = h


def _ple(h, p, g, wg, wp, fg, final):
    m, d = h.shape
    dp = p.shape[1]
    tm = min(ROW_TILE, m)
    row = lambda n: pl.BlockSpec((tm, n), lambda i: (i, 0))
    return pl.pallas_call(
        functools.partial(_ple_kernel, final=final),
        grid=(m // tm,),
        in_specs=[row(d), row(dp), _const_spec((1, d)), _const_spec((d, d)), _const_spec((dp, d)),
                  _const_spec((1, d))],
        out_specs=row(d),
        out_shape=jax.ShapeDtypeStruct((m, d), F32),
        compiler_params=_params("parallel"),
        name="ple",
    )(h, p, g, wg, wp, fg)


def _sb_kernel(q_ref, k_ref, v_ref, tri_ref, o_ref, q2_ref, zb_ref, run_ref, acc_ref, kmax_ref):
    tq, width = q_ref.shape[1], q_ref.shape[2]
    t = k_ref.shape[1]
    pairs = width // SB_PAIR
    qi = pl.program_id(1)
    lane = lax.broadcasted_iota(jnp.int32, (1, SB_PAIR), 1)
    first = lane < SB_HEAD_DIM
    ones = jnp.ones((SB_PAIR, SB_PAIR), BF16)
    pair_lanes = lambda p: slice(p * SB_PAIR, (p + 1) * SB_PAIR)
    pair_rows = lambda p: slice(2 * p * tq, (2 * p + 2) * tq)

    def head_split(x):
        zero = jnp.zeros_like(x)
        return jnp.where(first, x, zero), jnp.where(first, zero, x)

    def sq_norm(x):
        xf = x.astype(F32)
        return _dot((xf * xf).astype(BF16), ones)

    @pl.when(qi == 0)
    def _():
        for p in range(pairs):
            def body(i, m, p=p):
                rows = pl.ds(pl.multiple_of(i * SB_NORM_ROWS, SB_NORM_ROWS), SB_NORM_ROWS)
                ka, kb_ = head_split(k_ref[0, rows, pair_lanes(p)])
                return (jnp.maximum(m[0], jnp.max(sq_norm(ka), axis=0, keepdims=True)),
                        jnp.maximum(m[1], jnp.max(sq_norm(kb_), axis=0, keepdims=True)))
            zero = jnp.zeros((1, SB_PAIR), F32)
            ma, mb = lax.fori_loop(0, t // SB_NORM_ROWS, body, (zero, zero))
            kmax_ref[2 * p:2 * p + 1, :] = ma
            kmax_ref[2 * p + 1:2 * p + 2, :] = mb

    for p in range(pairs):
        q = q_ref[0, :, pair_lanes(p)] * jnp.asarray(SB_HEAD_DIM ** -0.5, BF16)
        for a, qh in enumerate(head_split(q)):
            h = 2 * p + a
            q2_ref[h * tq:(h + 1) * tq, :] = qh
            zb_ref[h * tq:(h + 1) * tq, :] = jnp.sqrt(sq_norm(qh) * kmax_ref[h:h + 1, :]) * SB_BOUND_SLACK

    row = lax.broadcasted_iota(jnp.int32, (2 * pairs * tq, tq), 0)
    col = lax.broadcasted_iota(jnp.int32, (2 * pairs * tq, tq), 1)
    strict = col < row % tq

    def step(kb, diagonal):
        off = pl.multiple_of(kb * tq, tq)
        z = jnp.concatenate(
            [_dot_nt(q2_ref[pair_rows(p), :], k_ref[0, pl.ds(off, tq), pair_lanes(p)]) for p in range(pairs)],
            axis=0)
        sp = _softplus(z)
        if diagonal:
            sp = jnp.where(strict, sp, 0.0)
        hi, lo = _split_bf16(sp)
        ct = _dot(jnp.concatenate([hi, lo], axis=1), tri_ref[...])
        if diagonal:
            w = jnp.where(strict, jnp.exp(z - ct[:, :tq]), 0.0).astype(BF16)
        else:
            w = jnp.exp(z - ct[:, :tq] + run_ref[...]).astype(BF16)
        for p in range(pairs):
            pv = _dot(w[pair_rows(p), :], v_ref[0, pl.ds(off, tq), pair_lanes(p)])
            if diagonal:
                acc_ref[pair_rows(p), :] = pv
            else:
                acc_ref[pair_rows(p), :] += pv
        if diagonal:
            run_ref[...] = -ct[:, tq:]
        else:
            run_ref[...] -= ct[:, tq:]
        return (jnp.max(run_ref[...] + zb_ref[...]) > SB_ZERO_LOG).astype(jnp.int32)

    def body(c):
        return c[0] + 1, step(qi - 1 - c[0], False)

    lax.while_loop(lambda c: jnp.logical_and(c[0] < qi, c[1] > 0), body, (jnp.int32(0), step(qi, True)))
    for p in range(pairs):
        o_ref[0, :, pair_lanes(p)] = jnp.where(
            first, acc_ref[2 * p * tq:(2 * p + 1) * tq, :], acc_ref[(2 * p + 1) * tq:(2 * p + 2) * tq, :]
        ).astype(o_ref.dtype)


def _sb_attention(qkv, width):
    b, t, _ = qkv.shape
    heads = width // SB_HEAD_DIM
    tq = min(SB_BLOCK, t)
    j = jnp.arange(2 * tq)[:, None] % tq
    s = jnp.arange(2 * tq)[None, :]
    tri = jnp.logical_or(s >= tq, j >= s).astype(BF16)
    stacked = lambda dt: pltpu.VMEM((heads * tq, SB_PAIR), dt)
    return pl.pallas_call(
        _sb_kernel,
        grid=(b, t // tq),
        in_specs=[pl.BlockSpec((1, tq, width), lambda bi, i: (bi, i, 0)),
                  pl.BlockSpec((1, t, width), lambda bi, i: (bi, 0, 1), pipeline_mode=pl.Buffered(1)),
                  pl.BlockSpec((1, t, width), lambda bi, i: (bi, 0, 2), pipeline_mode=pl.Buffered(1)),
                  _const_spec((2 * tq, 2 * tq))],
        out_specs=pl.BlockSpec((1, tq, width), lambda bi, i: (bi, i, 0)),
        out_shape=jax.ShapeDtypeStruct((b, t, width), BF16),
        scratch_shapes=[stacked(BF16), stacked(F32), stacked(F32), stacked(F32),
                        pltpu.VMEM((heads, SB_PAIR), F32)],
        compiler_params=_params("parallel", "arbitrary"),
        name="sb_attention",
    )(qkv, qkv, qkv, tri)


def _intra_layout():
    offsets, total = [], 0
    for s in range(CHUNK):
        offsets.append(total)
        total += CHUNK - (s // INTRA_GROUP) * INTRA_GROUP
    return offsets, total


_INTRA_OFFSETS, _INTRA_ROWS = _intra_layout()


def _gla_tile(q_s, k_s, v_s, g_s, o_s, st_ref, b_ref, e_ref, p_ref, heads):
    rows, lk = q_s.shape
    lv = v_s.shape[1]
    dk, dv = lk // heads, lv // heads
    r = lax.broadcasted_iota(jnp.int32, (CHUNK, CHUNK), 0)
    c = lax.broadcasted_iota(jnp.int32, (CHUNK, CHUNK), 1)
    tril = (c <= r).astype(BF16)
    head_of_k = lax.broadcasted_iota(jnp.int32, (lk, lv), 0) // dk
    head_of_v = lax.broadcasted_iota(jnp.int32, (lk, lv), 1) // dv
    spread = (head_of_k == head_of_v).astype(BF16)
    same_head = (lax.broadcasted_iota(jnp.int32, (lv, lk), 0) // dv
                 == lax.broadcasted_iota(jnp.int32, (lv, lk), 1) // dk)
    row_id = lax.broadcasted_iota(jnp.int32, (CHUNK, 1), 0)

    def chunk(ci, _):
        base = pl.multiple_of(ci * CHUNK, CHUNK)
        q = q_s[pl.ds(base, CHUNK), :]
        k = k_s[pl.ds(base, CHUNK), :]
        v = v_s[pl.ds(base, CHUNK), :]
        g = g_s[pl.ds(base, CHUNK), :]
        g_hi, g_lo = _split_bf16(g)
        b = _dot(tril, g_hi) + _dot(tril, g_lo)
        b_ref[...] = b
        b_last = b[CHUNK - 1:CHUNK, :]
        st = st_ref[...]
        o_inter = _dot_nt((q * jnp.exp(b)).astype(BF16), st.astype(BF16))
        k_dec = (k * jnp.exp(b_last - b)).astype(BF16)
        upd = _dot_tn(v.astype(BF16), k_dec)
        st_ref[...] = st * jnp.exp(b_last) + jnp.where(same_head, upd, 0.0)

        for s in range(CHUNK):
            r0 = (s // INTRA_GROUP) * INTRA_GROUP
            n = CHUNK - r0
            pair = jnp.exp(b[r0:, :] - b_ref[s:s + 1, :]) * (q[r0:, :] * k_s[pl.ds(base + s, 1), :])
            if s % INTRA_GROUP:
                pair = jnp.where(row_id[r0:, :] >= s, pair, 0.0)
            e_ref[_INTRA_OFFSETS[s]:_INTRA_OFFSETS[s] + n, :] = pair.astype(BF16)
        p_ref[...] = _dot(e_ref[...], spread)
        for rg in range(CHUNK // 8):
            acc = o_inter[8 * rg:8 * rg + 8, :]
            for s in range(8 * rg + 8):
                r0 = (s // INTRA_GROUP) * INTRA_GROUP
                at = _INTRA_OFFSETS[s] + 8 * rg - r0
                acc = acc + p_ref[at:at + 8, :] * v_s[pl.ds(base + s, 1), :]
            o_s[pl.ds(base + 8 * rg, 8), :] = acc
        return 0

    lax.fori_loop(0, rows // CHUNK, chunk, 0, unroll=CHUNK_UNROLL)


def _head_norm_gate(o, gain, gate, heads):
    lv = o.shape[1]
    dv = lv // heads
    same = (lax.broadcasted_iota(jnp.int32, (lv, lv), 0) // dv
            == lax.broadcasted_iota(jnp.int32, (lv, lv), 1) // dv).astype(BF16)
    ms = _dot_split(o * o, same) * (1.0 / dv)
    return o * lax.rsqrt(ms + EPS) * gain * _silu(gate)


def _hgrn_kernel(q_ref, f_ref, i_ref, gate_ref, logits_ref, gain_ref, o_ref,
                 q_s, k_s, v_s, g_s, o_s, st_ref, b_ref, e_ref, p_ref, *, layer):
    @pl.when(pl.program_id(1) == 0)
    def _():
        st_ref[...] = jnp.zeros_like(st_ref)

    fl = f_ref[0]
    if layer == 0:
        log_f = -_softplus(-fl)
    else:
        logits = logits_ref[...]
        depth = logits.shape[0]
        top = logits[0:1, :]
        for j in range(1, depth):
            top = jnp.maximum(top, logits[j:j + 1, :])
        ex = [jnp.exp(logits[j:j + 1, :] - top) for j in range(depth)]
        lb = sum(ex[1:layer + 1]) / sum(ex)
        log_f = jnp.log(lb + (1.0 - lb) * jax.nn.sigmoid(fl))
    q_s[...] = _silu(q_ref[0])
    k_s[...] = 1.0 - jnp.exp(log_f)
    v_s[...] = i_ref[0]
    g_s[...] = log_f
    _gla_tile(q_s, k_s, v_s, g_s, o_s, st_ref, b_ref, e_ref, p_ref, HG_HEADS)
    o_ref[0] = _head_norm_gate(o_s[...], gain_ref[...], gate_ref[0], HG_HEADS).astype(o_ref.dtype)


def _gla_kernel(q_ref, k_ref, v_ref, gate_ref, a_ref, wup_ref, bias_ref, gain_ref, o_ref,
                q_s, k_s, v_s, g_s, o_s, st_ref, b_ref, e_ref, p_ref):
    @pl.when(pl.program_id(1) == 0)
    def _():
        st_ref[...] = jnp.zeros_like(st_ref)

    dk =q_ref.shape[2] // GLA_HEADS
    pre = _dot(a_ref[0].astype(BF16), wup_ref[...]) + bias_ref[...]
    q_s[...] = q_ref[0] * (dk ** -0.5)
    k_s[...] = k_ref[0]
    v_s[...] = v_ref[0]
    g_s[...] = -_softplus(-pre) * (1.0 / GLA_GATE_TAU)
    _gla_tile(q_s, k_s, v_s, g_s, o_s, st_ref, b_ref, e_ref, p_ref, GLA_HEADS)
    o_ref[0] = _head_norm_gate(o_s[...], gain_ref[...], gate_ref[0], GLA_HEADS).astype(o_ref.dtype)


def _recurrent_scratch(tc, lk, lv):
    return [pltpu.VMEM((tc, lk), F32), pltpu.VMEM((tc, lk), F32), pltpu.VMEM((tc, lv), F32),
            pltpu.VMEM((tc, lk), F32), pltpu.VMEM((tc, lv), F32), pltpu.VMEM((lv, lk), F32),
            pltpu.VMEM((CHUNK, lk), F32), pltpu.VMEM((_INTRA_ROWS, lk), BF16),
            pltpu.VMEM((_INTRA_ROWS, lv), F32)]


def _hgrn(y, logits, gain, layer):
    b, t, w4 = y.shape
    w = w4 // 4
    tc = min(SEQ_TILE, t)
    col = lambda j: pl.BlockSpec((1, tc, w), lambda bi, i: (bi, i, j))
    return pl.pallas_call(
        functools.partial(_hgrn_kernel, layer=layer),
        grid=(b, t // tc),
        in_specs=[col(0), col(1), col(2), col(3), _const_spec(logits.shape), _const_spec((1, w))],
        out_specs=pl.BlockSpec((1, tc, w), lambda bi, i: (bi, i, 0)),
        out_shape=jax.ShapeDtypeStruct((b, t, w), BF16),
        scratch_shapes=_recurrent_scratch(tc, w, w),
        compiler_params=_params("parallel", "arbitrary"),
        name="hgrn2",
    )(y, y, y, y, logits, gain)


def _gla(y, wup, bias, gain, lk, lv):
    b, t, _ = y.shape
    tc = min(SEQ_TILE, t)
    blk = lambda n, j: pl.BlockSpec((1, tc, n), lambda bi, i: (bi, i, j))
    a_col = (2 * lk + 2 * lv) // GLA_GATE_PAD
    return pl.pallas_call(
        _gla_kernel,
        grid=(b, t // tc),
        in_specs=[blk(lk, 0), blk(lk, 1), blk(lv, (2 * lk) // lv), blk(lv, (2 * lk) // lv + 1),
                  blk(GLA_GATE_PAD, a_col), _const_spec(wup.shape), _const_spec((1, lk)),
                  _const_spec((1, lv))],
        out_specs=pl.BlockSpec((1, tc, lv), lambda bi, i: (bi, i, 0)),
        out_shape=jax.ShapeDtypeStruct((b, t, lv), BF16),
        scratch_shapes=_recurrent_scratch(tc, lk, lv),
        compiler_params=_params("parallel", "arbitrary"),
        name="gla",
    )(y, y, y, y, y, wup, bias, gain)


def kernel(x, p, ffn1_norm, ffn1_w_gate, ffn1_w_up, ffn1_w_down, mix_norm, w_in, w_out, hgrn_lb_logits, hgrn_out_gain, gla_w_gate_up, gla_b_gate, gla_out_gain, ffn2_norm, ffn2_w_gate, ffn2_w_up, ffn2_w_down, ple_norm, ple_w_gate, ple_w_proj, final_norm):
    bsz, t, d = x.shape
    depth = p.shape[0]
    m = bsz * t
    sb_w = d // 2
    hg_w = hgrn_out_gain.shape[1]
    gla_v = gla_out_gain.shape[1]
    gla_k = gla_w_gate_up.shape[2]
    rank = gla_w_gate_up.shape[1]
    bf = lambda a: a.astype(BF16)
    row = lambda a: a.reshape(1, -1)

    h = x.reshape(m, d)
    for i in range(depth):
        h = _ffn(h, row(ffn1_norm[i]), bf(ffn1_w_gate[i]), bf(ffn1_w_up[i]), bf(ffn1_w_down[i]))

        wi = w_in[i]
        c_sb, c_hg = 3 * sb_w, 3 * sb_w + 4 * hg_w
        c_v = c_hg + 2 * gla_k
        c_a = c_v + gla_v
        c_g = c_a + rank
        w_gla = jnp.concatenate(
            [wi[:, c_hg:c_a], wi[:, c_g:c_g + gla_v], wi[:, c_a:c_g],
             jnp.zeros((d, GLA_GATE_PAD - rank), wi.dtype)], axis=1)
        y_sb, y_hg, y_gla = _inproj(h, row(mix_norm[i]), bf(wi[:, :c_sb]), bf(wi[:, c_sb:c_hg]), bf(w_gla))

        sb = _sb_attention(y_sb.reshape(bsz, t, -1), sb_w)
        hg = _hgrn(y_hg.reshape(bsz, t, -1), hgrn_lb_logits, row(hgrn_out_gain[i]), i)
        wup = jnp.concatenate([gla_w_gate_up[i], jnp.zeros((GLA_GATE_PAD - rank, gla_k), F32)], axis=0)
        gl = _gla(y_gla.reshape(bsz, t, -1), bf(wup), row(gla_b_gate[i]), row(gla_out_gain[i]), gla_k, gla_v)

        wo = bf(w_out[i])
        h = _outproj(h, sb.reshape(m, -1), hg.reshape(m, -1), gl.reshape(m, -1),
                     wo[:sb_w], wo[sb_w:sb_w + hg_w], wo[sb_w + hg_w:])
        h = _ffn(h, row(ffn2_norm[i]), bf(ffn2_w_gate[i]), bf(ffn2_w_up[i]), bf(ffn2_w_down[i]))
        h = _ple(h, p[i].reshape(m, -1), row(ple_norm[i]), bf(ple_w_gate[i]), bf(ple_w_proj[i]),
                 row(final_norm), i == depth - 1)
    return h.reshape(bsz, t, d)
```

```python
import functools

import jax
import jax.numpy as jnp
from jax import lax
from jax.experimental import pallas as pl
from jax.experimental.pallas import tpu as pltpu

F32 = jnp.float32
BF16 = jnp.bfloat16

EPS = 1e-6
CHUNK = 32
CHUNK_UNROLL = 4
SB_HEAD_DIM = 64
SB_PAIR = 128
HG_HEADS = 4
GLA_HEADS = 4
GLA_GATE_TAU = 16.0
GLA_GATE_PAD = 128

V7X_VMEM_LIMIT_BYTES = 56 * 1024 * 1024
ROW_TILE = 512
FF_CHUNK = 256
SB_BLOCK = 128
SB_NORM_ROWS = 512
SB_BOUND_SLACK = 1.02
SB_ZERO_LOG = -105.0
SEQ_TILE = 512
INTRA_GROUP = 16


def _params(*sem):
    return pltpu.CompilerParams(dimension_semantics=sem, vmem_limit_bytes=V7X_VMEM_LIMIT_BYTES)


def _dot(a, b):
    return jnp.dot(a, b, preferred_element_type=F32)


def _dot_nt(a, b):
    return lax.dot_general(a, b, (((1,), (1,)), ((), ())), preferred_element_type=F32)


def _dot_tn(a, b):
    return lax.dot_general(a, b, (((0,), (0,)), ((), ())), preferred_element_type=F32)


def _split_bf16(x):
    hi = x.astype(BF16)
    lo = (x - hi.astype(F32)).astype(BF16)
    return hi, lo


def _dot_split(x, w):
    hi, lo = _split_bf16(x)
    return _dot(hi, w) + _dot(lo, w)


def _rms(x, g):
    ms = jnp.mean(x * x, axis=-1, keepdims=True)
    return x * lax.rsqrt(ms + EPS) * g


def _softplus(x):
    return jnp.maximum(x, 0.0) + jnp.log(1.0 + jnp.exp(-jnp.abs(x)))


def _silu(x):
    return x * jax.nn.sigmoid(x)


def _const_spec(shape):
    return pl.BlockSpec(shape, lambda *_: (0,) * len(shape), pipeline_mode=pl.Buffered(1))


def _ffn_kernel(*refs, mixed):
    if mixed:
        h_ref, sb_ref, hg_ref, gl_ref, w1_ref, w2_ref, w3_ref, g_ref, wg_ref, wu_ref, wd_ref, o_ref, a_ref = refs
        h = (h_ref[...] + _dot(sb_ref[...], w1_ref[...]) + _dot(hg_ref[...], w2_ref[...])
             + _dot(gl_ref[...], w3_ref[...]))
    else:
        h_ref, g_ref, wg_ref, wu_ref, wd_ref, o_ref, a_ref = refs
        h = h_ref[...]
    u = _rms(h, g_ref[...]).astype(BF16)
    for j in range(wg_ref.shape[1] // FF_CHUNK):
        sl = slice(j * FF_CHUNK, (j + 1) * FF_CHUNK)
        gate = _dot(u, wg_ref[:, sl])
        up = _dot(u, wu_ref[:, sl])
        a_ref[:, sl] = (_silu(gate) * up).astype(BF16)
    o_ref[...] = h + 0.5 * _dot(a_ref[...], wd_ref[...])


def _ffn(h, g, wg, wu, wd, mix=()):
    m, d = h.shape
    f = wg.shape[1]
    tm = min(ROW_TILE, m)
    row = lambda n: pl.BlockSpec((tm, n), lambda i: (i, 0))
    mix_specs = [row(a.shape[1]) for a in mix[:3]] + [_const_spec(w.shape) for w in mix[3:]]
    return pl.pallas_call(
        functools.partial(_ffn_kernel, mixed=bool(mix)),
        grid=(m // tm,),
        in_specs=[row(d)] + mix_specs
        + [_const_spec((1, d)), _const_spec((d, f)), _const_spec((d, f)), _const_spec((f, d))],
        out_specs=row(d),
        out_shape=jax.ShapeDtypeStruct((m, d), F32),
        scratch_shapes=[pltpu.VMEM((tm, f), BF16)],
        compiler_params=_params("parallel"),
        name="ffn",
    )(h, *mix, g, wg, wu, wd)


def _inproj_kernel(h_ref, g_ref, w1_ref, w2_ref, w3_ref, o1_ref, o2_ref, o3_ref):
    u = _rms(h_ref[...], g_ref[...]).astype(BF16)
    o1_ref[...] = _dot(u, w1_ref[...]).astype(BF16)
    o2_ref[...] = _dot(u, w2_ref[...])
    o3_ref[...] = _dot(u, w3_ref[...])


def _inproj(h, g, w1, w2, w3):
    m, d = h.shape
    tm = min(ROW_TILE, m)
    row = lambda n: pl.BlockSpec((tm, n), lambda i: (i, 0))
    n1, n2, n3 = w1.shape[1], w2.shape[1], w3.shape[1]
    return pl.pallas_call(
        _inproj_kernel,
        grid=(m // tm,),
        in_specs=[row(d), _const_spec((1, d)), _const_spec((d, n1)), _const_spec((d, n2)), _const_spec((d, n3))],
        out_specs=[row(n1), row(n2), row(n3)],
        out_shape=[jax.ShapeDtypeStruct((m, n1), BF16), jax.ShapeDtypeStruct((m, n2), F32),
                   jax.ShapeDtypeStruct((m, n3), F32)],
        compiler_params=_params("parallel"),
        name="inproj",
    )(h, g, w1, w2, w3)


def _ple_kernel(h_ref, p_ref, g_ref, wg_ref, wp_ref, fg_ref, o_ref, *, final):
    h = h_ref[...]
    u = _rms(h, g_ref[...]).astype(BF16)
    gate = jax.nn.sigmoid(_dot(u, wg_ref[...]))
    h = h + gate * _dot(p_ref[...].astype(BF16), wp_ref[...])
    if final:
        h = _rms(h, fg_ref[...])
    o_ref[...] = h


def _ple(h, p, g, wg, wp, fg, final):
    m, d = h.shape
    dp = p.shape[1]
    tm = min(ROW_TILE, m)
    row = lambda n: pl.BlockSpec((tm, n), lambda i: (i, 0))
    return pl.pallas_call(
        functools.partial(_ple_kernel, final=final),
        grid=(m // tm,),
        in_specs=[row(d), row(dp), _const_spec((1, d)), _const_spec((d, d)), _const_spec((dp, d)),
                  _const_spec((1, d))],
        out_specs=row(d),
        out_shape=jax.ShapeDtypeStruct((m, d), F32),
        compiler_params=_params("parallel"),
        name="ple",
    )(h, p, g, wg, wp, fg)


def _sb_kernel(q_ref, k_ref, v_ref, tri_ref, o_ref, q2_ref, zb_ref, run_ref, acc_ref, kmax_ref):
    tq, width = q_ref.shape[1], q_ref.shape[2]
    t = k_ref.shape[1]
    pairs = width // SB_PAIR
    qi = pl.program_id(1)
    lane = lax.broadcasted_iota(jnp.int32, (1, SB_PAIR), 1)
    first = lane < SB_HEAD_DIM
    ones = jnp.ones((SB_PAIR, SB_PAIR), BF16)
    pair_lanes = lambda p: slice(p * SB_PAIR, (p + 1) * SB_PAIR)
    pair_rows = lambda p: slice(2 * p * tq, (2 * p + 2) * tq)

    def head_split(x):
        zero = jnp.zeros_like(x)
        return jnp.where(first, x, zero), jnp.where(first, zero, x)

    def sq_norm(x):
        xf = x.astype(F32)
        return _dot((xf * xf).astype(BF16), ones)

    @pl.when(qi == 0)
    def _():
        for p in range(pairs):
            def body(i, m, p=p):
                rows = pl.ds(pl.multiple_of(i * SB_NORM_ROWS, SB_NORM_ROWS), SB_NORM_ROWS)
                ka, kb_ = head_split(k_ref[0, rows, pair_lanes(p)])
                return (jnp.maximum(m[0], jnp.max(sq_norm(ka), axis=0, keepdims=True)),
                        jnp.maximum(m[1], jnp.max(sq_norm(kb_), axis=0, keepdims=True)))
            zero = jnp.zeros((1, SB_PAIR), F32)
            ma, mb = lax.fori_loop(0, t // SB_NORM_ROWS, body, (zero, zero))
            kmax_ref[2 * p:2 * p + 1, :] = ma
            kmax_ref[2 * p + 1:2 * p + 2, :] = mb

    for p in range(pairs):
        q = q_ref[0, :, pair_lanes(p)] * jnp.asarray(SB_HEAD_DIM ** -0.5, BF16)
        for a, qh in enumerate(head_split(q)):
            h = 2 * p + a
            q2_ref[h * tq:(h + 1) * tq, :] = qh
            zb_ref[h * tq:(h + 1) * tq, :] = jnp.sqrt(sq_norm(qh) * kmax_ref[h:h + 1, :]) * SB_BOUND_SLACK

    row = lax.broadcasted_iota(jnp.int32, (2 * pairs * tq, tq), 0)
    col = lax.broadcasted_iota(jnp.int32, (2 * pairs * tq, tq), 1)
    strict = col < row % tq

    def step(kb, diagonal):
        off = pl.multiple_of(kb * tq, tq)
        z = jnp.concatenate(
            [_dot_nt(q2_ref[pair_rows(p), :], k_ref[0, pl.ds(off, tq), pair_lanes(p)]) for p in range(pairs)],
            axis=0)
        sp = _softplus(z)
        if diagonal:
            sp = jnp.where(strict, sp, 0.0)
        hi, lo = _split_bf16(sp)
        ct = _dot(jnp.concatenate([hi, lo], axis=1), tri_ref[...])
        if diagonal:
            w = jnp.where(strict, jnp.exp(z - ct[:, :tq]), 0.0).astype(BF16)
        else:
            w = jnp.exp(z - ct[:, :tq] + run_ref[...]).astype(BF16)
        for p in range(pairs):
            pv = _dot(w[pair_rows(p), :], v_ref[0, pl.ds(off, tq), pair_lanes(p)])
            if diagonal:
                acc_ref[pair_rows(p), :] = pv
            else:
                acc_ref[pair_rows(p), :] += pv
        if diagonal:
            run_ref[...] = -ct[:, tq:]
        else:
            run_ref[...] -= ct[:, tq:]
        return (jnp.max(run_ref[...] + zb_ref[...]) > SB_ZERO_LOG).astype(jnp.int32)

    def body(c):
        return c[0] + 1, step(qi - 1 - c[0], False)

    lax.while_loop(lambda c: jnp.logical_and(c[0] < qi, c[1] > 0), body, (jnp.int32(0), step(qi, True)))
    for p in range(pairs):
        o_ref[0, :, pair_lanes(p)] = jnp.where(
            first, acc_ref[2 * p * tq:(2 * p + 1) * tq, :], acc_ref[(2 * p + 1) * tq:(2 * p + 2) * tq, :]
        ).astype(o_ref.dtype)


def _sb_attention(qkv, width):
    b, t, _ = qkv.shape
    heads = width // SB_HEAD_DIM
    tq = min(SB_BLOCK, t)
    j = jnp.arange(2 * tq)[:, None] % tq
    s = jnp.arange(2 * tq)[None, :]
    tri = jnp.logical_or(s >= tq, j >= s).astype(BF16)
    stacked = lambda dt: pltpu.VMEM((heads * tq, SB_PAIR), dt)
    return pl.pallas_call(
        _sb_kernel,
        grid=(b, t // tq),
        in_specs=[pl.BlockSpec((1, tq, width), lambda bi, i: (bi, i, 0)),
                  pl.BlockSpec((1, t, width), lambda bi, i: (bi, 0, 1), pipeline_mode=pl.Buffered(1)),
                  pl.BlockSpec((1, t, width), lambda bi, i: (bi, 0, 2), pipeline_mode=pl.Buffered(1)),
                  _const_spec((2 * tq, 2 * tq))],
        out_specs=pl.BlockSpec((1, tq, width), lambda bi, i: (bi, i, 0)),
        out_shape=jax.ShapeDtypeStruct((b, t, width), BF16),
        scratch_shapes=[stacked(BF16), stacked(F32), stacked(F32), stacked(F32),
                        pltpu.VMEM((heads, SB_PAIR), F32)],
        compiler_params=_params("parallel", "arbitrary"),
        name="sb_attention",
    )(qkv, qkv, qkv, tri)


def _intra_layout():
    offsets, total = [], 0
    for s in range(CHUNK):
        offsets.append(total)
        total += CHUNK - (s // INTRA_GROUP) * INTRA_GROUP
    return offsets, total


_INTRA_OFFSETS, _INTRA_ROWS = _intra_layout()


def _gla_tile(q_s, k_s, v_s, g_s, o_s, st_ref, b_ref, e_ref, p_ref, heads):
    rows, lk = q_s.shape
    lv = v_s.shape[1]
    dk, dv = lk // heads, lv // heads
    r = lax.broadcasted_iota(jnp.int32, (CHUNK, CHUNK), 0)
    c = lax.broadcasted_iota(jnp.int32, (CHUNK, CHUNK), 1)
    tril = (c <= r).astype(BF16)
    head_of_k = lax.broadcasted_iota(jnp.int32, (lk, lv), 0) // dk
    head_of_v = lax.broadcasted_iota(jnp.int32, (lk, lv), 1) // dv
    spread = (head_of_k == head_of_v).astype(BF16)
    same_head = (lax.broadcasted_iota(jnp.int32, (lv, lk), 0) // dv
                 == lax.broadcasted_iota(jnp.int32, (lv, lk), 1) // dk)
    row_id = lax.broadcasted_iota(jnp.int32, (CHUNK, 1), 0)

    def chunk(ci, _):
        base = pl.multiple_of(ci * CHUNK, CHUNK)
        q = q_s[pl.ds(base, CHUNK), :]
        k = k_s[pl.ds(base, CHUNK), :]
        v = v_s[pl.ds(base, CHUNK), :]
        g = g_s[pl.ds(base, CHUNK), :]
        g_hi, g_lo = _split_bf16(g)
        b = _dot(tril, g_hi) + _dot(tril, g_lo)
        b_ref[...] = b
        b_last = b[CHUNK - 1:CHUNK, :]
        st = st_ref[...]
        o_inter = _dot_nt((q * jnp.exp(b)).astype(BF16), st.astype(BF16))
        k_dec = (k * jnp.exp(b_last - b)).astype(BF16)
        upd = _dot_tn(v.astype(BF16), k_dec)
        st_ref[...] = st * jnp.exp(b_last) + jnp.where(same_head, upd, 0.0)

        for s in range(CHUNK):
            r0 = (s // INTRA_GROUP) * INTRA_GROUP
            n = CHUNK - r0
            pair = jnp.exp(b[r0:, :] - b_ref[s:s + 1, :]) * (q[r0:, :] * k_s[pl.ds(base + s, 1), :])
            if s % INTRA_GROUP:
                pair = jnp.where(row_id[r0:, :] >= s, pair, 0.0)
            e_ref[_INTRA_OFFSETS[s]:_INTRA_OFFSETS[s] + n, :] = pair.astype(BF16)
        p_ref[...] = _dot(e_ref[...], spread)
        for rg in range(CHUNK // 8):
            acc = o_inter[8 * rg:8 * rg + 8, :]
            for s in range(8 * rg + 8):
                r0 = (s // INTRA_GROUP) * INTRA_GROUP
                at = _INTRA_OFFSETS[s] + 8 * rg - r0
                acc = acc + p_ref[at:at + 8, :] * v_s[pl.ds(base + s, 1), :]
            o_s[pl.ds(base + 8 * rg, 8), :] = acc
        return 0

    lax.fori_loop(0, rows // CHUNK, chunk, 0, unroll=CHUNK_UNROLL)


def _head_norm_gate(o, gain, gate, heads):
    lv = o.shape[1]
    dv = lv // heads
    same = (lax.broadcasted_iota(jnp.int32, (lv, lv), 0) // dv
            == lax.broadcasted_iota(jnp.int32, (lv, lv), 1) // dv).astype(BF16)
    ms = _dot_split(o * o, same) * (1.0 / dv)
    return o * lax.rsqrt(ms + EPS) * gain * _silu(gate)


def _hgrn_kernel(q_ref, f_ref, i_ref, gate_ref, logits_ref, gain_ref, o_ref,
                 q_s, k_s, v_s, g_s, o_s, st_ref, b_ref, e_ref, p_ref, *, layer):
    @pl.when(pl.program_id(1) == 0)
    def _():
        st_ref[...] = jnp.zeros_like(st_ref)

    fl = f_ref[0]
    if layer == 0:
        log_f = -_softplus(-fl)
    else:
        logits = logits_ref[...]
        depth = logits.shape[0]
        top = logits[0:1, :]
        for j in range(1, depth):
            top = jnp.maximum(top, logits[j:j + 1, :])
        ex = [jnp.exp(logits[j:j + 1, :] - top) for j in range(depth)]
        lb = sum(ex[1:layer + 1]) / sum(ex)
        log_f = jnp.log(lb + (1.0 - lb) * jax.nn.sigmoid(fl))
    q_s[...] = _silu(q_ref[0])
    k_s[...] = 1.0 - jnp.exp(log_f)
    v_s[...] = i_ref[0]
    g_s[...] = log_f
    _gla_tile(q_s, k_s, v_s, g_s, o_s, st_ref, b_ref, e_ref, p_ref, HG_HEADS)
    o_ref[0] = _head_norm_gate(o_s[...], gain_ref[...], gate_ref[0], HG_HEADS).astype(o_ref.dtype)


def _gla_kernel(q_ref, k_ref, v_ref, gate_ref, a_ref, wup_ref, bias_ref, gain_ref, o_ref,
                q_s, k_s, v_s, g_s, o_s, st_ref, b_ref, e_ref, p_ref):
    @pl.when(pl.program_id(1) == 0)
    def _():
        st_ref[...] = jnp.zeros_like(st_ref)

    dk =q_ref.shape[2] // GLA_HEADS
    pre = _dot(a_ref[0].astype(BF16), wup_ref[...]) + bias_ref[...]
    q_s[...] = q_ref[0] * (dk ** -0.5)
    k_s[...] = k_ref[0]
    v_s[...] = v_ref[0]
    g_s[...] = -_softplus(-pre) * (1.0 / GLA_GATE_TAU)
    _gla_tile(q_s, k_s, v_s, g_s, o_s, st_ref, b_ref, e_ref, p_ref, GLA_HEADS)
    o_ref[0] = _head_norm_gate(o_s[...], gain_ref[...], gate_ref[0], GLA_HEADS).astype(o_ref.dtype)


def _recurrent_scratch(tc, lk, lv):
    return [pltpu.VMEM((tc, lk), F32), pltpu.VMEM((tc, lk), F32), pltpu.VMEM((tc, lv), F32),
            pltpu.VMEM((tc, lk), F32), pltpu.VMEM((tc, lv), F32), pltpu.VMEM((lv, lk), F32),
            pltpu.VMEM((CHUNK, lk), F32), pltpu.VMEM((_INTRA_ROWS, lk), BF16),
            pltpu.VMEM((_INTRA_ROWS, lv), F32)]


def _hgrn(y, logits, gain, layer):
    b, t, w4 = y.shape
    w = w4 // 4
    tc = min(SEQ_TILE, t)
    col = lambda j: pl.BlockSpec((1, tc, w), lambda bi, i: (bi, i, j))
    return pl.pallas_call(
        functools.partial(_hgrn_kernel, layer=layer),
        grid=(b, t // tc),
        in_specs=[col(0), col(1), col(2), col(3), _const_spec(logits.shape), _const_spec((1, w))],
        out_specs=pl.BlockSpec((1, tc, w), lambda bi, i: (bi, i, 0)),
        out_shape=jax.ShapeDtypeStruct((b, t, w), BF16),
        scratch_shapes=_recurrent_scratch(tc, w, w),
        compiler_params=_params("parallel", "arbitrary"),
        name="hgrn2",
    )(y, y, y, y, logits, gain)


def _gla(y, wup, bias, gain, lk, lv):
    b, t, _ = y.shape
    tc = min(SEQ_TILE, t)
    blk = lambda n, j: pl.BlockSpec((1, tc, n), lambda bi, i: (bi, i, j))
    a_col = (2 * lk + 2 * lv) // GLA_GATE_PAD
    return pl.pallas_call(
        _gla_kernel,
        grid=(b, t // tc),
        in_specs=[blk(lk, 0), blk(lk, 1), blk(lv, (2 * lk) // lv), blk(lv, (2 * lk) // lv + 1),
                  blk(GLA_GATE_PAD, a_col), _const_spec(wup.shape), _const_spec((1, lk)),
                  _const_spec((1, lv))],
        out_specs=pl.BlockSpec((1, tc, lv), lambda bi, i: (bi, i, 0)),
        out_shape=jax.ShapeDtypeStruct((b, t, lv), BF16),
        scratch_shapes=_recurrent_scratch(tc, lk, lv),
        compiler_params=_params("parallel", "arbitrary"),
        name="gla",
    )(y, y, y, y, y, wup, bias, gain)


def kernel(x, p, ffn1_norm, ffn1_w_gate, ffn1_w_up, ffn1_w_down, mix_norm, w_in, w_out, hgrn_lb_logits, hgrn_out_gain, gla_w_gate_up, gla_b_gate, gla_out_gain, ffn2_norm, ffn2_w_gate, ffn2_w_up, ffn2_w_down, ple_norm, ple_w_gate, ple_w_proj, final_norm):
    bsz, t, d = x.shape
    depth = p.shape[0]
    m = bsz * t
    sb_w = d // 2
    hg_w = hgrn_out_gain.shape[1]
    gla_v = gla_out_gain.shape[1]
    gla_k = gla_w_gate_up.shape[2]
    rank = gla_w_gate_up.shape[1]
    bf = lambda a: a.astype(BF16)
    row = lambda a: a.reshape(1, -1)

    h = x.reshape(m, d)
    for i in range(depth):
        h = _ffn(h, row(ffn1_norm[i]), bf(ffn1_w_gate[i]), bf(ffn1_w_up[i]), bf(ffn1_w_down[i]))

        wi = w_in[i]
        c_sb, c_hg = 3 * sb_w, 3 * sb_w + 4 * hg_w
        c_v = c_hg + 2 * gla_k
        c_a = c_v + gla_v
        c_g = c_a + rank
        w_gla = jnp.concatenate(
            [wi[:, c_hg:c_a], wi[:, c_g:c_g + gla_v], wi[:, c_a:c_g],
             jnp.zeros((d, GLA_GATE_PAD - rank), wi.dtype)], axis=1)
        y_sb, y_hg, y_gla = _inproj(h, row(mix_norm[i]), bf(wi[:, :c_sb]), bf(wi[:, c_sb:c_hg]), bf(w_gla))

        sb = _sb_attention(y_sb.reshape(bsz, t, -1), sb_w)
        hg = _hgrn(y_hg.reshape(bsz, t, -1), hgrn_lb_logits, row(hgrn_out_gain[i]), i)
        wup = jnp.concatenate([gla_w_gate_up[i], jnp.zeros((GLA_GATE_PAD - rank, gla_k), F32)], axis=0)
        gl = _gla(y_gla.reshape(bsz, t, -1), bf(wup), row(gla_b_gate[i]), row(gla_out_gain[i]), gla_k, gla_v)

        wo = bf(w_out[i])
        mix = (sb.reshape(m, -1), hg.reshape(m, -1), gl.reshape(m, -1),
               wo[:sb_w], wo[sb_w:sb_w + hg_w], wo[sb_w + hg_w:])
        h = _ffn(h, row(ffn2_norm[i]), bf(ffn2_w_gate[i]), bf(ffn2_w_up[i]), bf(ffn2_w_down[i]), mix)
        h = _ple(h, p[i].reshape(m, -1), row(ple_norm[i]), bf(ple_w_gate[i]), bf(ple_w_proj[i]),
                 row(final_norm), i == depth - 1)
    return h.reshape(bsz, t, d)
```
